```python
import math
import jax, jax.numpy as jnp
from jax import lax
import numpy as np

D_MODEL = 4096
BATCH = 4
SEQ = 2048
DEPTH = 1
DEC_BATCH = 32
DEC_SEQ = 8
PAST_LEN = 8192
PAGE_SIZE = 128

D_MIX = D_MODEL
D_ATTN = D_MIX // 2
D_SSM = D_MIX - D_ATTN
ATTN_VHEAD = 128
ATTN_SUB = ATTN_VHEAD // 2
N_HEADS = D_ATTN // ATTN_VHEAD
N_KV_HEADS = max(1, N_HEADS // 4)
ROT_DIM = ATTN_SUB // 4
ROPE_THETA = 500000.0
Q_BLOCK = 128
SSM_HEAD_DIM = 64
N_SSM_HEADS = D_SSM // SSM_HEAD_DIM
N_SSM_GROUPS = 8
SSM_STATE = 128
CONV_WIDTH = 4
CONV_DIM = D_SSM + 2 * N_SSM_GROUPS * SSM_STATE
SSD_CHUNK = 128
D_FF = 4 * D_MODEL
EPS = 1e-6
D_Q = N_HEADS * 2 * ATTN_SUB
D_K = N_KV_HEADS * 2 * ATTN_SUB
D_V = N_KV_HEADS * ATTN_VHEAD
D_IN = D_Q + D_K + D_V + D_SSM + CONV_DIM + N_SSM_HEADS

kernel_name = "hymba_diffattn_ssd_decode_step"

F32 = jnp.float32


def rms_norm(x, w):
    x32 = x.astype(F32)
    y = x32 * lax.rsqrt(jnp.mean(x32 * x32, axis=-1, keepdims=True) + EPS)
    return (y * w.astype(F32)).astype(x.dtype)


def partial_rope(x, pos):
    half = ROT_DIM // 2
    inv = jnp.exp(-math.log(ROPE_THETA) * jnp.arange(half, dtype=F32) * 2.0 / ROT_DIM)
    ang = pos.astype(F32)[:, None] * inv[None, :]
    cos = jnp.cos(ang)[None, :, None, None, :]
    sin = jnp.sin(ang)[None, :, None, None, :]
    x32 = x.astype(F32)
    x1, x2 = x32[..., :half], x32[..., half:ROT_DIM]
    out = jnp.concatenate([x1 * cos - x2 * sin, x2 * cos + x1 * sin, x32[..., ROT_DIM:]], axis=-1)
    return out.astype(x.dtype)


def diff_attention(q, k, v, q_pos, k_pos, lam):
    b, lq = q.shape[0], q.shape[1]
    grp = N_HEADS // N_KV_HEADS
    qb = min(Q_BLOCK, lq)
    pad = (-lq) % qb
    q = q.reshape(b, lq, N_KV_HEADS, grp, 2, ATTN_SUB)
    if pad:
        q = jnp.pad(q, ((0, 0), (0, pad), (0, 0), (0, 0), (0, 0), (0, 0)))
        q_pos = jnp.pad(q_pos, (0, pad), mode="edge")
    nb = (lq + pad) // qb
    q_blocks = q.reshape(b, nb, qb, N_KV_HEADS, grp, 2, ATTN_SUB).transpose(1, 0, 2, 3, 4, 5, 6)
    pos_blocks = q_pos.reshape(nb, qb)
    scale = ATTN_SUB ** -0.5

    def one_block(args):
        qblk, qp = args
        s = jnp.einsum("bqkgcd,bskcd->bkgcqs", qblk, k, preferred_element_type=F32) * scale
        mask = k_pos[None, :] <= qp[:, None]
        s = jnp.where(mask, s, -jnp.inf)
        p = jax.nn.softmax(s, axis=-1)
        a = p[:, :, :, 0] - lam * p[:, :, :, 1]
        return jnp.einsum("bkgqs,bskd->bqkgd", a.astype(v.dtype), v)

    out = lax.map(one_block, (q_blocks, pos_blocks))
    out = out.transpose(1, 0, 2, 3, 4, 5).reshape(b, nb * qb, N_HEADS, ATTN_VHEAD)
    return out[:, :lq]


def causal_conv(xbc, prev, w, bias):
    xp = jnp.concatenate([prev.astype(xbc.dtype), xbc], axis=1)
    out = lax.conv_general_dilated(xp, w[:, None, :].astype(xbc.dtype), window_strides=(1,),
                                   padding="VALID", dimension_numbers=("NWC", "WIO", "NWC"),
                                   feature_group_count=CONV_DIM)
    return out + bias.astype(xbc.dtype), xp[:, -(CONV_WIDTH - 1):]


def ssd_chunked(x, dt, a, bm, cm, state0):
    b, L, h, p = x.shape
    cl = min(SSD_CHUNK, L)
    pad = (-L) % cl
    x = x.astype(F32)
    bm = jnp.repeat(bm.astype(F32), h // N_SSM_GROUPS, axis=2)
    cm = jnp.repeat(cm.astype(F32), h // N_SSM_GROUPS, axis=2)
    if pad:
        pw = ((0, 0), (0, pad), (0, 0), (0, 0))
        x, bm, cm = jnp.pad(x, pw), jnp.pad(bm, pw), jnp.pad(cm, pw)
        dt = jnp.pad(dt, ((0, 0), (0, pad), (0, 0)))
    nc = (L + pad) // cl
    xdt = (x * dt[..., None]).reshape(b, nc, cl, h, p)
    bm = bm.reshape(b, nc, cl, h, SSM_STATE)
    cm = cm.reshape(b, nc, cl, h, SSM_STATE)
    cs = jnp.cumsum((dt * a).reshape(b, nc, cl, h), axis=2)
    seg = cs[:, :, :, None, :] - cs[:, :, None, :, :]
    tri = jnp.tril(jnp.ones((cl, cl), dtype=bool))[None, None, :, :, None]
    lmat = jnp.exp(jnp.where(tri, seg, -jnp.inf))
    scores = jnp.einsum("bclhn,bcshn->bclsh", cm, bm) * lmat
    y_diag = jnp.einsum("bclsh,bcshp->bclhp", scores, xdt)
    decay_to_end = jnp.exp(cs[:, :, -1:, :] - cs)
    chunk_states = jnp.einsum("bclhn,bclh,bclhp->bchpn", bm, decay_to_end, xdt)
    chunk_decay = jnp.exp(cs[:, :, -1, :])

    def step(s, inp):
        dec, cst = inp
        return dec[:, :, None, None] * s + cst, s

    final, prev = lax.scan(step, state0.astype(F32),
                           (chunk_decay.transpose(1, 0, 2), chunk_states.transpose(1, 0, 2, 3, 4)))
    prev = prev.transpose(1, 0, 2, 3, 4)
    y_off = jnp.einsum("bclhn,bchpn,bclh->bclhp", cm, prev, jnp.exp(cs))
    y = (y_diag + y_off).reshape(b, nc * cl, h, p)[:, :L]
    return y, final


def decoder_layer(x, pos, k_past, v_past, past_pos, conv_prev, ssm_prev,
                  ln1_w, w_in, q_norm_w, k_norm_w, lambda_q1, lambda_k1, lambda_q2, lambda_k2,
                  subln_w, conv_w, conv_b, dt_bias, a_log, d_skip, ssm_norm_w, w_out,
                  ln2_w, w_up, w_down, lam_init):
    b, L, _ = x.shape
    h = rms_norm(x, ln1_w)
    proj = h @ w_in
    idx = [int(i) for i in np.cumsum([D_Q, D_K, D_V, D_SSM, CONV_DIM])]
    q, k, v, z, xbc, dt = jnp.split(proj, idx, axis=-1)

    q = partial_rope(rms_norm(q.reshape(b, L, N_HEADS, 2, ATTN_SUB), q_norm_w), pos)
    k = partial_rope(rms_norm(k.reshape(b, L, N_KV_HEADS, 2, ATTN_SUB), k_norm_w), pos)
    v = v.reshape(b, L, N_KV_HEADS, ATTN_VHEAD)
    k_rows = k.reshape(b, L, N_KV_HEADS, 2 * ATTN_SUB)
    if k_past is None:
        k_all, v_all, k_pos = k, v, pos
    else:
        k_all = jnp.concatenate([k_past.reshape(b, -1, N_KV_HEADS, 2, ATTN_SUB).astype(k.dtype), k], axis=1)
        v_all = jnp.concatenate([v_past.reshape(b, -1, N_KV_HEADS, ATTN_VHEAD).astype(v.dtype), v], axis=1)
        k_pos = jnp.concatenate([past_pos, pos])
    lam = (jnp.exp(jnp.sum(lambda_q1.astype(F32) * lambda_k1.astype(F32)))
           - jnp.exp(jnp.sum(lambda_q2.astype(F32) * lambda_k2.astype(F32))) + lam_init)
    o = diff_attention(q, k_all, v_all, pos, k_pos, lam)
    o = (rms_norm(o, subln_w) * (1.0 - lam_init)).reshape(b, L, D_ATTN)

    xbc, conv_new = causal_conv(xbc, conv_prev, conv_w, conv_b)
    xbc = jax.nn.silu(xbc)
    xs, bm, cm = jnp.split(xbc, [D_SSM, D_SSM + N_SSM_GROUPS * SSM_STATE], axis=-1)
    xs = xs.reshape(b, L, N_SSM_HEADS, SSM_HEAD_DIM)
    bm = bm.reshape(b, L, N_SSM_GROUPS, SSM_STATE)
    cm = cm.reshape(b, L, N_SSM_GROUPS, SSM_STATE)
    dtp = jax.nn.softplus(dt.astype(F32) + dt_bias.astype(F32))
    a = -jnp.exp(a_log.astype(F32))
    y, ssm_new = ssd_chunked(xs, dtp, a, bm, cm, ssm_prev)
    y = y + d_skip.astype(F32)[:, None] * xs.astype(F32)
    y = y.reshape(b, L, D_SSM) * jax.nn.silu(z.astype(F32))
    gs = D_SSM // N_SSM_GROUPS
    y = rms_norm(y.reshape(b, L, N_SSM_GROUPS, gs), ssm_norm_w.reshape(N_SSM_GROUPS, gs)).reshape(b, L, D_SSM)

    x = x + jnp.concatenate([o, y.astype(o.dtype)], axis=-1) @ w_out
    x = x + jnp.square(jax.nn.relu(rms_norm(x, ln2_w) @ w_up)) @ w_down
    return x, k_rows, v, ssm_new.astype(ssm_prev.dtype), conv_new


def setup_inputs(seed: int = 0) -> dict:
    key = jax.random.key(seed)
    ks = jax.random.split(key, 32)
    n_pages = PAST_LEN // PAGE_SIZE
    n_used = DEC_BATCH * n_pages
    n_pool = n_used + max(1, n_used // 4)
    nrm = lambda k, shape, s: jax.random.normal(k, shape, F32) * s
    page_table = jax.random.permutation(ks[0], n_pool)[:n_used].reshape(DEC_BATCH, n_pages).astype(jnp.int32)
    dt0 = jnp.exp(jax.random.uniform(ks[1], (DEPTH, N_SSM_HEADS), F32, math.log(1e-3), math.log(1e-1)))
    return {
        "x_prompt": nrm(ks[2], (BATCH, SEQ, D_MODEL), 1.0),
        "x_sample": nrm(ks[3], (DEC_BATCH, DEC_SEQ, D_MODEL), 1.0),
        "cache_k": nrm(ks[4], (DEPTH, n_pool, PAGE_SIZE, N_KV_HEADS, 2 * ATTN_SUB), 1.0),
        "cache_v": nrm(ks[5], (DEPTH, n_pool, PAGE_SIZE, N_KV_HEADS, ATTN_VHEAD), 1.0),
        "state_ssm": nrm(ks[6], (DEPTH, DEC_BATCH, N_SSM_HEADS, SSM_HEAD_DIM, SSM_STATE), 0.1),
        "state_conv": nrm(ks[7], (DEPTH, DEC_BATCH, CONV_WIDTH - 1, CONV_DIM), 1.0),
        "page_table": page_table,
        "ln1_w": 1.0 + nrm(ks[8], (DEPTH, D_MODEL), 0.02),
        "w_in": nrm(ks[9], (DEPTH, D_MODEL, D_IN), D_MODEL ** -0.5),
        "q_norm_w": 1.0 + nrm(ks[10], (DEPTH, ATTN_SUB), 0.02),
        "k_norm_w": 1.0 + nrm(ks[11], (DEPTH, ATTN_SUB), 0.02),
        "lambda_q1": nrm(ks[12], (DEPTH, ATTN_SUB), 0.1),
        "lambda_k1": nrm(ks[13], (DEPTH, ATTN_SUB), 0.1),
        "lambda_q2": nrm(ks[14], (DEPTH, ATTN_SUB), 0.1),
        "lambda_k2": nrm(ks[15], (DEPTH, ATTN_SUB), 0.1),
        "subln_w": 1.0 + nrm(ks[16], (DEPTH, ATTN_VHEAD), 0.02),
        "conv_w": nrm(ks[17], (DEPTH, CONV_WIDTH, CONV_DIM), CONV_WIDTH ** -0.5),
        "conv_b": nrm(ks[18], (DEPTH, CONV_DIM), 0.02),
        "dt_bias": dt0 + jnp.log(-jnp.expm1(-dt0)),
        "a_log": jnp.log(jax.random.uniform(ks[19], (DEPTH, N_SSM_HEADS), F32, 1.0, 16.0)),
        "d_skip": 1.0 + nrm(ks[20], (DEPTH, N_SSM_HEADS), 0.1),
        "ssm_norm_w": 1.0 + nrm(ks[21], (DEPTH, D_SSM), 0.02),
        "w_out": nrm(ks[22], (DEPTH, D_MIX, D_MODEL), D_MIX ** -0.5),
        "ln2_w": 1.0 + nrm(ks[23], (DEPTH, D_MODEL), 0.02),
        "w_up": nrm(ks[24], (DEPTH, D_MODEL, D_FF), D_MODEL ** -0.5),
        "w_down": nrm(ks[25], (DEPTH, D_FF, D_MODEL), D_FF ** -0.5),
    }


def reference(x_prompt, x_sample, cache_k, cache_v, state_ssm, state_conv, page_table,
              ln1_w, w_in, q_norm_w, k_norm_w, lambda_q1, lambda_k1, lambda_q2, lambda_k2,
              subln_w, conv_w, conv_b, dt_bias, a_log, d_skip, ssm_norm_w, w_out,
              ln2_w, w_up, w_down):
    past_len = page_table.shape[1] * cache_k.shape[2]
    bp, seq = x_prompt.shape[0], x_prompt.shape[1]
    dseq = x_sample.shape[1]
    pos_p = jnp.arange(seq, dtype=jnp.int32)
    pos_s = past_len + jnp.arange(dseq, dtype=jnp.int32)
    past_pos = jnp.arange(past_len, dtype=jnp.int32)
    yp, ys = x_prompt, x_sample
    kp_l, vp_l, sp_l, cp_l, ks_l, vs_l, ss_l, cs_l = [], [], [], [], [], [], [], []
    for l in range(DEPTH):
        lam_init = 0.8 - 0.6 * math.exp(-0.3 * l)
        lw = (ln1_w[l], w_in[l], q_norm_w[l], k_norm_w[l], lambda_q1[l], lambda_k1[l],
              lambda_q2[l], lambda_k2[l], subln_w[l], conv_w[l], conv_b[l], dt_bias[l],
              a_log[l], d_skip[l], ssm_norm_w[l], w_out[l], ln2_w[l], w_up[l], w_down[l])
        conv0 = jnp.zeros((bp, CONV_WIDTH - 1, CONV_DIM), x_prompt.dtype)
        ssm0 = jnp.zeros((bp, N_SSM_HEADS, SSM_HEAD_DIM, SSM_STATE), state_ssm.dtype)
        yp, kp, vp, sp, cp = decoder_layer(yp, pos_p, None, None, None, conv0, ssm0, *lw, lam_init)
        k_past = cache_k[l][page_table]
        v_past = cache_v[l][page_table]
        ys, ks_, vs_, ss_, cs_ = decoder_layer(ys, pos_s, k_past, v_past, past_pos,
                                               state_conv[l], state_ssm[l], *lw, lam_init)
        kp_l.append(kp); vp_l.append(vp); sp_l.append(sp); cp_l.append(cp)
        ks_l.append(ks_); vs_l.append(vs_); ss_l.append(ss_); cs_l.append(cs_)
    k_prompt, v_prompt = jnp.stack(kp_l), jnp.stack(vp_l)
    ssm_prompt, conv_prompt = jnp.stack(sp_l), jnp.stack(cp_l)
    k_sample, v_sample = jnp.stack(ks_l), jnp.stack(vs_l)
    ssm_sample, conv_sample = jnp.stack(ss_l), jnp.stack(cs_l)
    return (yp, ys, k_prompt, v_prompt, ssm_prompt, conv_prompt, k_sample, v_sample, ssm_sample, conv_sample)
```

```python
import functools
import math

import jax
import jax.numpy as jnp
from jax import lax
from jax.experimental import pallas as pl
from jax.experimental.pallas import tpu as pltpu

F32 = jnp.float32
BF16 = jnp.bfloat16

EPS = 1e-6
LANES = 128
SUBLANES = 8
VMEM_LIMIT = 56 * 1024 * 1024

ATTN_VHEAD = 128
ATTN_SUB = 64
ROT_DIM = 16
ROPE_THETA = 500000.0
KV_GROUP = 4
SSM_HEAD_DIM = 64
SSM_STATE = 128
N_SSM_GROUPS = 8
CONV_WIDTH = 4
SSD_CHUNK = 128
NEG_INF = float("-inf")


def _cparams(sem):
    return pltpu.CompilerParams(dimension_semantics=sem, vmem_limit_bytes=VMEM_LIMIT)


def _tile(n, target, mult=1, also=()):
    t = min(n, target)
    while t > 0:
        if n % t == 0 and t % mult == 0 and all(a % t == 0 for a in also):
            return t
        t -= 1
    raise ValueError((n, target, mult, also))


def _rmsnorm_kernel(x_ref, w_ref, o_ref):
    x = x_ref[...]
    ms = jnp.mean(x * x, axis=-1, keepdims=True)
    o_ref[...] = (x * lax.rsqrt(ms + EPS) * w_ref[...]).astype(o_ref.dtype)


def _rmsnorm(x, w, out_dtype=BF16):
    m, d = x.shape
    tr = _tile(m, 256, SUBLANES)
    return pl.pallas_call(
        _rmsnorm_kernel,
        out_shape=jax.ShapeDtypeStruct((m, d), out_dtype),
        grid=(m // tr,),
        in_specs=[pl.BlockSpec((tr, d), lambda i: (i, 0)),
                  pl.BlockSpec((1, d), lambda i: (0, 0))],
        out_specs=pl.BlockSpec((tr, d), lambda i: (i, 0)),
        compiler_params=_cparams(("parallel",)),
        name="rmsnorm",
    )(x, w.reshape(1, d))


def _mm_kernel(*refs, n_pairs, has_res, nk, act):
    pairs = refs[:2 * n_pairs]
    pos = 2 * n_pairs
    res_ref = refs[pos] if has_res else None
    o_ref = refs[pos + int(has_res)]
    part = None
    for p in range(n_pairs):
        x = pairs[2 * p][...]
        if x.dtype != BF16:
            x = x.astype(BF16)
        d = jnp.dot(x, pairs[2 * p + 1][...], preferred_element_type=F32)
        part = d if part is None else part + d

    if nk == 1:
        if act == "relu2":
            part = jnp.square(jnp.maximum(part, 0.0))
        if has_res:
            part = res_ref[...] + part
        o_ref[...] = part.astype(o_ref.dtype)
    else:
        k = pl.program_id(2)

        @pl.when(k == 0)
        def _():
            o_ref[...] = (res_ref[...] + part) if has_res else part

        @pl.when(k > 0)
        def _():
            o_ref[...] += part


def _matmul(pairs, *, n_out, w_col0=0, res=None, act=None, out_dtype=F32,
            tm=1024, tn=1024, tk=None, name="matmul"):
    m, kdim = pairs[0][0].shape
    tm = _tile(m, tm, SUBLANES)
    tn = _tile(n_out, tn, LANES, also=(w_col0,))
    tk = kdim if tk is None else _tile(kdim, tk, LANES)
    nk = kdim // tk
    if nk > 1:
        assert act is None and out_dtype == F32
    c0 = w_col0 // tn
    in_specs, args = [], []
    for x, w in pairs:
        assert x.shape == (m, kdim) and w.shape[0] == kdim
        in_specs += [pl.BlockSpec((tm, tk), lambda i, j, k: (i, k)),
                     pl.BlockSpec((tk, tn), lambda i, j, k: (k, j + c0))]
        args += [x, w]
    if res is not None:
        in_specs.append(pl.BlockSpec((tm, tn), lambda i, j, k: (i, j)))
        args.append(res)
    kern = functools.partial(_mm_kernel, n_pairs=len(pairs), has_res=res is not None, nk=nk, act=act)
    return pl.pallas_call(
        kern,
        out_shape=jax.ShapeDtypeStruct((m, n_out), out_dtype),
        grid=(m // tm, n_out // tn, nk),
        in_specs=in_specs,
        out_specs=pl.BlockSpec((tm, tn), lambda i, j, k: (i, j)),
        compiler_params=_cparams(("parallel", "parallel", "arbitrary")),
        name=name,
    )(*args)


def _qkv_post_kernel(qkv_ref, cos_ref, sa_ref, sb_ref, qw_ref, kw_ref,
                     q_ref, kf_ref, kb_ref, vf_ref, vb_ref, *, d_q, d_k):
    tr = qkv_ref.shape[0]
    cos, sa, sb = cos_ref[...], sa_ref[...], sb_ref[...]
    lane = lax.broadcasted_iota(jnp.int32, (tr, LANES), 1)
    lo = lane < ATTN_SUB

    def norm_rope(x, w):
        x2 = x * x
        s_lo = jnp.sum(jnp.where(lo, x2, 0.0), axis=-1, keepdims=True)
        s_hi = jnp.sum(jnp.where(lo, 0.0, x2), axis=-1, keepdims=True)
        ms = jnp.where(lo, s_lo, s_hi) * (1.0 / ATTN_SUB)
        y = x * lax.rsqrt(ms + EPS) * w
        up = pltpu.roll(y, LANES - ROT_DIM // 2, 1)
        dn = pltpu.roll(y, ROT_DIM // 2, 1)
        return y * cos + up * sa + dn * sb

    qw, kw = qw_ref[...], kw_ref[...]
    scale = ATTN_SUB ** -0.5
    for g in range(d_q // LANES):
        sl = slice(g * LANES, (g + 1) * LANES)
        q_ref[:, sl] = (norm_rope(qkv_ref[:, sl], qw) * scale).astype(q_ref.dtype)
    for g in range(d_k // LANES):
        y = norm_rope(qkv_ref[:, d_q + g * LANES:d_q + (g + 1) * LANES], kw)
        sl = slice(g * LANES, (g + 1) * LANES)
        kf_ref[:, sl] = y
        kb_ref[:, sl] = y.astype(BF16)
    v = qkv_ref[:, d_q + d_k:]
    vf_ref[...] = v
    vb_ref[...] = v.astype(BF16)


def _rope_tables(pos):
    half = ROT_DIM // 2
    inv = jnp.exp(-math.log(ROPE_THETA) * jnp.arange(half, dtype=F32) * 2.0 / ROT_DIM)
    ang = pos.astype(F32)[:, None] * inv[None, :]
    cos, sin = jnp.cos(ang), jnp.sin(ang)
    n = pos.shape[0]
    ones = jnp.ones((n, ATTN_SUB - ROT_DIM), F32)
    zeros8 = jnp.zeros((n, half), F32)
    zeros = jnp.zeros((n, ATTN_SUB - ROT_DIM), F32)
    c = jnp.concatenate([cos, cos, ones], axis=1)
    sa = jnp.concatenate([-sin, zeros8, zeros], axis=1)
    sb = jnp.concatenate([zeros8, sin, zeros], axis=1)
    rep = LANES // ATTN_SUB
    return jnp.tile(c, (1, rep)), jnp.tile(sa, (1, rep)), jnp.tile(sb, (1, rep))


def _qkv_post(qkv, tables, q_norm_w, k_norm_w, *, d_q, d_k, q_dtype):
    m = qkv.shape[0]
    d_v = qkv.shape[1] - d_q - d_k
    n_tab = tables[0].shape[0]
    tr = _tile(m, 256, SUBLANES, also=(n_tab,))
    ntb = n_tab // tr
    rep = LANES // ATTN_SUB
    row = lambda i: (i, 0)
    tab = lambda i: (i % ntb, 0)
    fix = lambda i: (0, 0)
    kern = functools.partial(_qkv_post_kernel, d_q=d_q, d_k=d_k)
    return pl.pallas_call(
        kern,
        out_shape=(jax.ShapeDtypeStruct((m, d_q), q_dtype),
                   jax.ShapeDtypeStruct((m, d_k), F32),
                   jax.ShapeDtypeStruct((m, d_k), BF16),
                   jax.ShapeDtypeStruct((m, d_v), F32),
                   jax.ShapeDtypeStruct((m, d_v), BF16)),
        grid=(m // tr,),
        in_specs=[pl.BlockSpec((tr, qkv.shape[1]), row),
                  pl.BlockSpec((tr, LANES), tab), pl.BlockSpec((tr, LANES), tab),
                  pl.BlockSpec((tr, LANES), tab),
                  pl.BlockSpec((1, LANES), fix), pl.BlockSpec((1, LANES), fix)],
        out_specs=(pl.BlockSpec((tr, d_q), row), pl.BlockSpec((tr, d_k), row),
                   pl.BlockSpec((tr, d_k), row), pl.BlockSpec((tr, d_v), row),
                   pl.BlockSpec((tr, d_v), row)),
        compiler_params=_cparams(("parallel",)),
        name="qkv_post",
    )(qkv, *tables, jnp.tile(q_norm_w, rep).reshape(1, LANES), jnp.tile(k_norm_w, rep).reshape(1, LANES))


def _lambda_value(lam_ref, lam_init):
    l = lam_ref[...]
    d1 = jnp.sum(l[0:1] * l[1:2], axis=-1, keepdims=True)
    d2 = jnp.sum(l[2:3] * l[3:4], axis=-1, keepdims=True)
    return jnp.exp(d1) - jnp.exp(d2) + lam_init


def _subln(o, w, lam_init):
    ms = jnp.mean(o * o, axis=-1, keepdims=True)
    return (o * lax.rsqrt(ms + EPS) * w) * (1.0 - lam_init)


def _dot_nt(a, b):
    return lax.dot_general(a, b, (((1,), (1,)), ((), ())), preferred_element_type=F32)


def _attn_prompt_kernel(q_ref, k_ref, v_ref, lam_ref, subw_ref, o_ref, m_scr, l_scr, acc_scr,
                        *, tq, tk, lam_init):
    i, j = pl.program_id(2), pl.program_id(3)
    last_j = (i * tq + tq - 1) // tk
    n_rep = tk // LANES

    @pl.when(j == 0)
    def _():
        m_scr[...] = jnp.full(m_scr.shape, NEG_INF, F32)
        l_scr[...] = jnp.zeros(l_scr.shape, F32)
        acc_scr[...] = jnp.zeros(acc_scr.shape, F32)

    @pl.when(j <= last_j)
    def _():
        k = k_ref[...]
        v = v_ref[...]
        qpos = i * tq + lax.broadcasted_iota(jnp.int32, (tq, tk), 0)
        kpos = j * tk + lax.broadcasted_iota(jnp.int32, (tq, tk), 1)
        causal = kpos <= qpos
        lo = lax.broadcasted_iota(jnp.int32, (tq, LANES), 1) < ATTN_SUB
        zero = jnp.zeros((tq, LANES), BF16)
        for h in range(KV_GROUP):
            qh = q_ref[:, h * LANES:(h + 1) * LANES]
            for c in range(2):
                idx = 2 * h + c
                qm = jnp.where(lo, qh, zero) if c == 0 else jnp.where(lo, zero, qh)
                s = jnp.where(causal, _dot_nt(qm, k), NEG_INF)
                m_old = m_scr[idx]
                m_new = jnp.maximum(m_old, jnp.max(s, axis=-1, keepdims=True))
                p = jnp.exp(s - jnp.tile(m_new, (1, n_rep)))
                alpha = jnp.exp(m_old - m_new)
                l_scr[idx] = alpha * l_scr[idx] + jnp.sum(p, axis=-1, keepdims=True)
                acc_scr[idx] = alpha * acc_scr[idx] + jnp.dot(p.astype(BF16), v, preferred_element_type=F32)
                m_scr[idx] = m_new

    @pl.when(j == last_j)
    def _():
        lam = _lambda_value(lam_ref, lam_init)
        w = subw_ref[...]
        for h in range(KV_GROUP):
            o0 = acc_scr[2 * h] / l_scr[2 * h]
            o1 = acc_scr[2 * h + 1] / l_scr[2 * h + 1]
            o = _subln(o0 - lam * o1, w, lam_init)
            o_ref[:, h * LANES:(h + 1) * LANES] = o.astype(o_ref.dtype)


def _attn_prompt(q, k, v, lam_rows, subln_w, *, batch, seq, lam_init):
    m, d_q = q.shape
    n_kv = k.shape[1] // LANES
    tq = _tile(seq, 256, 16)
    tk = _tile(seq, 512, LANES)
    nq, nk = seq // tq, seq // tk
    gw = KV_GROUP * LANES

    def kv_map(b, g, i, j):
        return (b * nk + jnp.minimum(j, (i * tq + tq - 1) // tk), g)

    kern = functools.partial(_attn_prompt_kernel, tq=tq, tk=tk, lam_init=lam_init)
    return pl.pallas_call(
        kern,
        out_shape=jax.ShapeDtypeStruct((m, d_q), BF16),
        grid=(batch, n_kv, nq, nk),
        in_specs=[pl.BlockSpec((tq, gw), lambda b, g, i, j: (b * nq + i, g)),
                  pl.BlockSpec((tk, LANES), kv_map),
                  pl.BlockSpec((tk, LANES), kv_map),
                  pl.BlockSpec((4, ATTN_SUB), lambda b, g, i, j: (0, 0)),
                  pl.BlockSpec((1, LANES), lambda b, g, i, j: (0, 0))],
        out_specs=pl.BlockSpec((tq, gw), lambda b, g, i, j: (b * nq + i, g)),
        scratch_shapes=[pltpu.VMEM((2 * KV_GROUP, tq, LANES), F32),
                        pltpu.VMEM((2 * KV_GROUP, tq, LANES), F32),
                        pltpu.VMEM((2 * KV_GROUP, tq, LANES), F32)],
        compiler_params=_cparams(("parallel", "parallel", "parallel", "arbitrary")),
        name="attn_prompt",
    )(q, k, v, lam_rows, subln_w.reshape(1, LANES))


def _attn_decode_kernel(pt_ref, q_ref, kn_ref, vn_ref, lam_ref, subw_ref, *rest,
                        pps, n_steps, dseq, n_kv, lam_init):
    k_refs = rest[:pps]
    v_refs = rest[pps:2 * pps]
    o_ref = rest[2 * pps]
    qf_scr, qb_scr, kc_scr, vc_scr, m_scr, l_scr, acc_scr = rest[2 * pps + 1:]
    del pt_ref
    j = pl.program_id(1)
    page = k_refs[0].shape[0]
    rows = qf_scr.shape[0]
    blk = 2 * dseq

    @pl.when(j == 0)
    def _():
        qf_scr[...] = jnp.zeros(qf_scr.shape, F32)
        lo = lax.broadcasted_iota(jnp.int32, (dseq, LANES), 1) < ATTN_SUB
        for g in range(n_kv):
            for h in range(KV_GROUP):
                hd = g * KV_GROUP + h
                qh = q_ref[:, hd * LANES:(hd + 1) * LANES]
                r0 = hd * blk
                qf_scr[r0:r0 + dseq, g * LANES:(g + 1) * LANES] = jnp.where(lo, qh, 0.0)
                qf_scr[r0 + dseq:r0 + blk, g * LANES:(g + 1) * LANES] = jnp.where(lo, 0.0, qh)
        qb_scr[...] = qf_scr[...].astype(BF16)
        m_scr[...] = jnp.full(m_scr.shape, NEG_INF, F32)
        l_scr[...] = jnp.zeros(l_scr.shape, F32)
        acc_scr[...] = jnp.zeros(acc_scr.shape, F32)

    def online_update(s, v):
        m_old = m_scr[...]
        m_new = jnp.maximum(m_old, jnp.max(s, axis=-1, keepdims=True))
        p = jnp.exp(s - jnp.tile(m_new, (1, s.shape[1] // LANES)))
        alpha = jnp.exp(m_old - m_new)
        l_scr[...] = alpha * l_scr[...] + jnp.sum(p, axis=-1, keepdims=True)
        pv = jnp.dot(p.astype(BF16), v, preferred_element_type=F32)
        acc_scr[...] = jnp.tile(alpha, (1, n_kv)) * acc_scr[...] + pv
        m_scr[...] = m_new

    @pl.when(j < n_steps)
    def _():
        for t in range(pps):
            kc_scr[t * page:(t + 1) * page, :] = k_refs[t][...].astype(BF16)
            vc_scr[t * page:(t + 1) * page, :] = v_refs[t][...].astype(BF16)
        online_update(_dot_nt(qb_scr[...], kc_scr[...]), vc_scr[...])

    @pl.when(j == n_steps)
    def _():
        kpad = jnp.concatenate([kn_ref[...], jnp.zeros((page - dseq, kn_ref.shape[1]), F32)], axis=0)
        vpad = jnp.concatenate([vn_ref[...], jnp.zeros((page - dseq, vn_ref.shape[1]), F32)], axis=0)
        s = _dot_nt(qb_scr[...], kpad.astype(BF16))
        t_idx = lax.broadcasted_iota(jnp.int32, (rows, page), 0) % dseq
        u_idx = lax.broadcasted_iota(jnp.int32, (rows, page), 1)
        s = jnp.where(u_idx <= t_idx, s, NEG_INF)
        online_update(s, vpad.astype(BF16))

        lam = _lambda_value(lam_ref, lam_init)
        w = subw_ref[...]
        for g in range(n_kv):
            for h in range(KV_GROUP):
                hd = g * KV_GROUP + h
                r0 = hd * blk
                cs = slice(g * LANES, (g + 1) * LANES)
                o0 = acc_scr[r0:r0 + dseq, cs] / l_scr[r0:r0 + dseq, :]
                o1 = acc_scr[r0 + dseq:r0 + blk, cs] / l_scr[r0 + dseq:r0 + blk, :]
                o_ref[:, hd * LANES:(hd + 1) * LANES] = _subln(o0 - lam * o1, w, lam_init)


def _attn_decode(q, k_new, v_new, cache_k, cache_v, page_table, lam_rows, subln_w, *, dseq, lam_init):
    m, d_q = q.shape
    n_b, n_pages = page_table.shape
    n_pool, page, kvw = cache_k.shape
    n_kv = kvw // LANES
    pps = _tile(n_pages, 8)
    n_steps = n_pages // pps
    rows = n_kv * KV_GROUP * 2 * dseq

    def page_map(t):
        def f(b, j, pt):
            return (pt[b, jnp.minimum(j, n_steps - 1) * pps + t], 0, 0)
        return f

    tok = lambda b, j, pt: (b, 0)
    fix = lambda b, j, pt: (0, 0)
    kv_specs = lambda: [pl.BlockSpec((None, page, kvw), page_map(t)) for t in range(pps)]
    kern = functools.partial(_attn_decode_kernel, pps=pps, n_steps=n_steps, dseq=dseq, n_kv=n_kv,
                             lam_init=lam_init)
    return pl.pallas_call(
        kern,
        out_shape=jax.ShapeDtypeStruct((m, d_q), F32),
        grid_spec=pltpu.PrefetchScalarGridSpec(
            num_scalar_prefetch=1,
            grid=(n_b, n_steps + 1),
            in_specs=[pl.BlockSpec((dseq, d_q), tok),
                      pl.BlockSpec((dseq, kvw), tok), pl.BlockSpec((dseq, kvw), tok),
                      pl.BlockSpec((4, ATTN_SUB), fix), pl.BlockSpec((1, LANES), fix)]
                     + kv_specs() + kv_specs(),
            out_specs=pl.BlockSpec((dseq, d_q), tok),
            scratch_shapes=[pltpu.VMEM((rows, kvw), F32), pltpu.VMEM((rows, kvw), BF16),
                            pltpu.VMEM((pps * page, kvw), BF16), pltpu.VMEM((pps * page, kvw), BF16),
                            pltpu.VMEM((rows, LANES), F32), pltpu.VMEM((rows, LANES), F32),
                            pltpu.VMEM((rows, kvw), F32)]),
        compiler_params=_cparams(("parallel", "arbitrary")),
        name="attn_decode",
    )(page_table, q, k_new, v_new, lam_rows, subln_w.reshape(1, LANES),
      *([cache_k] * pps), *([cache_v] * pps))


def _silu(x):
    return x * (1.0 / (1.0 + jnp.exp(-x)))


def _conv_taps(xp, w, bias, n_rows):
    acc = None
    for t in range(CONV_WIDTH):
        shift = CONV_WIDTH - 1 - t
        xs = xp if shift == 0 else pltpu.roll(xp, shift, 0)
        term = xs[SUBLANES:] * w[t:t + 1]
        acc = term if acc is None else acc + term
    return _silu(acc + bias)


def _conv_prompt_kernel(x_ref, halo_ref, w_ref, b_ref, o_ref):
    i = pl.program_id(1)
    halo = jnp.where(i > 0, halo_ref[...], 0.0)
    xp = jnp.concatenate([halo, x_ref[...]], axis=0)
    o_ref[...] = _conv_taps(xp, w_ref[...], b_ref[...], x_ref.shape[0])


def _conv_prompt(xbc, conv_w, conv_b, *, batch, seq):
    m, c = xbc.shape
    tr = _tile(seq, 256, SUBLANES)
    tc = _tile(c, 1024, LANES)
    nr = seq // tr
    hb = tr // SUBLANES
    return pl.pallas_call(
        _conv_prompt_kernel,
        out_shape=jax.ShapeDtypeStruct((m, c), F32),
        grid=(batch, nr, c // tc),
        in_specs=[pl.BlockSpec((tr, tc), lambda b, i, j: (b * nr + i, j)),
                  pl.BlockSpec((SUBLANES, tc), lambda b, i, j: (jnp.maximum((b * nr + i) * hb - 1, 0), j)),
                  pl.BlockSpec((CONV_WIDTH, tc), lambda b, i, j: (0, j)),
                  pl.BlockSpec((1, tc), lambda b, i, j: (0, j))],
        out_specs=pl.BlockSpec((tr, tc), lambda b, i, j: (b * nr + i, j)),
        compiler_params=_cparams(("parallel", "parallel", "parallel")),
        name="conv_prompt",
    )(xbc, xbc, conv_w, conv_b.reshape(1, c))


def _cumsum_rows(x):
    n = x.shape[0]
    row = lax.broadcasted_iota(jnp.int32, x.shape, 0)
    s = 1
    while s < n:
        x = x + jnp.where(row >= s, pltpu.roll(x, s, 0), 0.0)
        s *= 2
    return x


def _ssd_chunk(xs, bm, cm, dt_raw, z, state_ref, dtb, alog, dskip, normw, *, n_valid):
    cl, d_ssm = xs.shape
    n_heads = d_ssm // SSM_HEAD_DIM
    hpg = n_heads // N_SSM_GROUPS
    gs = d_ssm // N_SSM_GROUPS
    x_dt = dt_raw + dtb
    dtp = jnp.maximum(x_dt, 0.0) + jnp.log1p(jnp.exp(-jnp.abs(x_dt)))
    if n_valid < cl:
        dtp = jnp.where(lax.broadcasted_iota(jnp.int32, dtp.shape, 0) < n_valid, dtp, 0.0)
    a = -jnp.exp(alog)
    cs = _cumsum_rows(dtp * a)
    cs_t = cs.T
    cs_last = cs[cl - 1:cl, :]
    dec_end = jnp.exp(cs_last - cs)
    dec_in = jnp.exp(cs)
    dec_all = jnp.exp(cs_last)
    tri = lax.broadcasted_iota(jnp.int32, (cl, cl), 0) >= lax.broadcasted_iota(jnp.int32, (cl, cl), 1)
    lo = lax.broadcasted_iota(jnp.int32, (cl, LANES), 1) < SSM_HEAD_DIM

    def pair_cols(arr, h0):
        return jnp.where(lo, arr[:, h0:h0 + 1], arr[:, h0 + 1:h0 + 2])

    y_parts = []
    for g in range(N_SSM_GROUPS):
        bg = bm[:, g * SSM_STATE:(g + 1) * SSM_STATE]
        cg = cm[:, g * SSM_STATE:(g + 1) * SSM_STATE]
        bg16, cg16 = bg.astype(BF16), cg.astype(BF16)
        cb = _dot_nt(cg16, bg16)
        for hp in range(hpg // 2):
            h0 = g * hpg + 2 * hp
            col = slice(h0 * SSM_HEAD_DIM, (h0 + 2) * SSM_HEAD_DIM)
            x_pair = xs[:, col]
            xdt = x_pair * pair_cols(dtp, h0)
            xdt16 = xdt.astype(BF16)
            halves = []
            for e in range(2):
                h = h0 + e
                seg = cs[:, h:h + 1] - cs_t[h:h + 1, :]
                lmat = jnp.exp(jnp.where(tri, seg, NEG_INF))
                scores = (cb * lmat).astype(BF16)
                halves.append(jnp.dot(scores, xdt16, preferred_element_type=F32))
            y_diag = jnp.where(lo, halves[0], halves[1])
            rows = slice(h0 * SSM_HEAD_DIM, (h0 + 2) * SSM_HEAD_DIM)
            st = state_ref[rows, :]
            y_off = _dot_nt(cg16, st.astype(BF16)) * pair_cols(dec_in, h0)
            y_parts.append(y_diag + y_off + pair_cols(dskip, h0) * x_pair)
            xdec = (xdt * pair_cols(dec_end, h0)).astype(BF16)
            new = lax.dot_general(xdec, bg16, (((0,), (0,)), ((), ())), preferred_element_type=F32)
            hsel = lax.broadcasted_iota(jnp.int32, (2 * SSM_HEAD_DIM, SSM_STATE), 0) < SSM_HEAD_DIM
            dec = jnp.where(hsel, dec_all[:, h0:h0 + 1], dec_all[:, h0 + 1:h0 + 2])
            state_ref[rows, :] = dec * st + new
    y = jnp.concatenate(y_parts, axis=1) * _silu(z)
    outs = []
    for g in range(N_SSM_GROUPS):
        yg = y[:, g * gs:(g + 1) * gs]
        ms = jnp.mean(yg * yg, axis=-1, keepdims=True)
        outs.append(yg * lax.rsqrt(ms + EPS) * normw[:, g * gs:(g + 1) * gs])
    return jnp.concatenate(outs, axis=1)


def _ssd_prompt_kernel(xs_ref, b_ref, c_ref, dt_ref, z_ref, dtb_ref, alog_ref, dskip_ref, normw_ref,
                       y_ref, st_ref):
    @pl.when(pl.program_id(1) == 0)
    def _():
        st_ref[...] = jnp.zeros(st_ref.shape, F32)

    y = _ssd_chunk(xs_ref[...], b_ref[...], c_ref[...], dt_ref[...], z_ref[...], st_ref,
                   dtb_ref[...], alog_ref[...], dskip_ref[...], normw_ref[...], n_valid=xs_ref.shape[0])
    y_ref[...] = y.astype(y_ref.dtype)


def _pad_lanes(v):
    return jnp.pad(v, (0, LANES - v.shape[0])).reshape(1, LANES)


def _ssd_prompt(xbc_act, dt, z_src, z_col0, dt_bias, a_log, d_skip, ssm_norm_w, *, batch, seq, d_ssm):
    m = xbc_act.shape[0]
    cl = min(SSD_CHUNK, seq)
    nc = seq // cl
    gn = N_SSM_GROUPS * SSM_STATE
    n_heads = d_ssm // SSM_HEAD_DIM
    row = lambda b, c: (b * nc + c, 0)
    fix = lambda b, c: (0, 0)
    assert d_ssm % gn == 0 and z_col0 % d_ssm == 0
    return pl.pallas_call(
        _ssd_prompt_kernel,
        out_shape=(jax.ShapeDtypeStruct((m, d_ssm), BF16),
                   jax.ShapeDtypeStruct((batch, n_heads * SSM_HEAD_DIM, SSM_STATE), F32)),
        grid=(batch, nc),
        in_specs=[pl.BlockSpec((cl, d_ssm), row),
                  pl.BlockSpec((cl, gn), lambda b, c: (b * nc + c, d_ssm // gn)),
                  pl.BlockSpec((cl, gn), lambda b, c: (b * nc + c, d_ssm // gn + 1)),
                  pl.BlockSpec((cl, LANES), row),
                  pl.BlockSpec((cl, d_ssm), lambda b, c: (b * nc + c, z_col0 // d_ssm)),
                  pl.BlockSpec((1, LANES), fix), pl.BlockSpec((1, LANES), fix),
                  pl.BlockSpec((1, LANES), fix), pl.BlockSpec((1, d_ssm), fix)],
        out_specs=(pl.BlockSpec((cl, d_ssm), row),
                   pl.BlockSpec((None, n_heads * SSM_HEAD_DIM, SSM_STATE), lambda b, c: (b, 0, 0))),
        compiler_params=_cparams(("parallel", "arbitrary")),
        name="ssd_prompt",
    )(xbc_act, xbc_act, xbc_act, dt, z_src, _pad_lanes(dt_bias), _pad_lanes(a_log), _pad_lanes(d_skip),
      ssm_norm_w.reshape(1, d_ssm))


def _ssd_decode_kernel(xbc_ref, prev_ref, dt_ref, z_ref, st0_ref, cw_ref, cb_ref, dtb_ref, alog_ref,
                       dskip_ref, normw_ref, y_ref, st_ref, *, dseq, d_ssm, cl):
    gn = N_SSM_GROUPS * SSM_STATE
    prev = prev_ref[...]
    c = prev.shape[1]
    xp = jnp.concatenate([jnp.zeros((SUBLANES - (CONV_WIDTH - 1), c), F32), prev, xbc_ref[...]], axis=0)
    act = _conv_taps(xp, cw_ref[...], cb_ref[...], dseq)
    pad = lambda v: jnp.concatenate([v, jnp.zeros((cl - dseq, v.shape[1]), F32)], axis=0)
    act = pad(act)
    st_ref[...] = st0_ref[...]
    y = _ssd_chunk(act[:, :d_ssm], act[:, d_ssm:d_ssm + gn], act[:, d_ssm + gn:], pad(dt_ref[...]),
                   pad(z_ref[...]), st_ref, dtb_ref[...], alog_ref[...], dskip_ref[...], normw_ref[...],
                   n_valid=dseq)
    y_ref[...] = y[:dseq]


def _ssd_decode(xbc, conv_prev, dt, z_src, z_col0, state0, conv_w, conv_b, dt_bias, a_log, d_skip,
                ssm_norm_w, *, dseq, d_ssm):
    m, c = xbc.shape
    n_b = m // dseq
    n_heads = d_ssm // SSM_HEAD_DIM
    st_rows = n_heads * SSM_HEAD_DIM
    cl = SSD_CHUNK
    tok = lambda b: (b, 0)
    fix = lambda b: (0, 0)
    st = lambda b: (b, 0, 0)
    assert z_col0 % d_ssm == 0
    kern = functools.partial(_ssd_decode_kernel, dseq=dseq, d_ssm=d_ssm, cl=cl)
    return pl.pallas_call(
        kern,
        out_shape=(jax.ShapeDtypeStruct((m, d_ssm), F32),
                   jax.ShapeDtypeStruct((n_b, st_rows, SSM_STATE), F32)),
        grid=(n_b,),
        in_specs=[pl.BlockSpec((dseq, c), tok),
                  pl.BlockSpec((None, CONV_WIDTH - 1, c), st),
                  pl.BlockSpec((dseq, LANES), tok),
                  pl.BlockSpec((dseq, d_ssm), lambda b: (b, z_col0 // d_ssm)),
                  pl.BlockSpec((None, st_rows, SSM_STATE), st),
                  pl.BlockSpec((CONV_WIDTH, c), fix), pl.BlockSpec((1, c), fix),
                  pl.BlockSpec((1, LANES), fix), pl.BlockSpec((1, LANES), fix),
                  pl.BlockSpec((1, LANES), fix), pl.BlockSpec((1, d_ssm), fix)],
        out_specs=(pl.BlockSpec((dseq, d_ssm), tok),
                   pl.BlockSpec((None, st_rows, SSM_STATE), st)),
        compiler_params=_cparams(("parallel",)),
        name="ssd_decode",
    )(xbc, conv_prev, dt, z_src, state0.reshape(n_b, st_rows, SSM_STATE), conv_w, conv_b.reshape(1, c),
      _pad_lanes(dt_bias), _pad_lanes(a_log), _pad_lanes(d_skip), ssm_norm_w.reshape(1, d_ssm))


def _layer(x, *, batch, seq, pos, wts, lam_init, decode=None):
    m, d_model = x.shape
    d_attn = d_model // 2
    d_ssm = d_model - d_attn
    n_heads = d_attn // ATTN_VHEAD
    n_kv = max(1, n_heads // KV_GROUP)
    d_q, d_k, d_v = n_heads * LANES, n_kv * LANES, n_kv * ATTN_VHEAD
    conv_dim = d_ssm + 2 * N_SSM_GROUPS * SSM_STATE
    tm = 1024 if decode is None else 256
    c_z = d_q + d_k + d_v
    c_x = c_z + d_ssm

    h = _rmsnorm(x, wts["ln1_w"])
    qkv = _matmul([(h, wts["w_in"])], n_out=c_z, w_col0=0, tm=tm, name="in_proj_qkv")
    zed = _matmul([(h, wts["w_in"])], n_out=d_ssm, w_col0=c_z, tm=tm, name="in_proj_z")
    xbc = _matmul([(h, wts["w_in"])], n_out=conv_dim, w_col0=c_x, tm=tm, name="in_proj_xbc")
    dt = _matmul([(h, wts["w_dt"])], n_out=LANES, tm=tm, name="in_proj_dt")

    tables = _rope_tables(pos)
    if decode is not None:
        tables = tuple(jnp.tile(t, (batch, 1)) for t in tables)
    q, k_f32, k_b16, v_f32, v_b16 = _qkv_post(
        qkv, tables, wts["q_norm_w"], wts["k_norm_w"], d_q=d_q, d_k=d_k,
        q_dtype=BF16 if decode is None else F32)
    lam_rows = jnp.stack([wts["lambda_q1"], wts["lambda_k1"], wts["lambda_q2"], wts["lambda_k2"]])

    if decode is None:
        o = _attn_prompt(q, k_b16, v_b16, lam_rows, wts["subln_w"], batch=batch, seq=seq, lam_init=lam_init)
        act = _conv_prompt(xbc, wts["conv_w"], wts["conv_b"], batch=batch, seq=seq)
        y, ssm_new = _ssd_prompt(act, dt, zed, 0, wts["dt_bias"], wts["a_log"], wts["d_skip"],
                                 wts["ssm_norm_w"], batch=batch, seq=seq, d_ssm=d_ssm)
        conv_new = xbc.reshape(batch, seq, conv_dim)[:, seq - (CONV_WIDTH - 1):]
    else:
        o = _attn_decode(q, k_f32, v_f32, decode["cache_k"], decode["cache_v"], decode["page_table"],
                         lam_rows, wts["subln_w"], dseq=seq, lam_init=lam_init)
        y, ssm_new = _ssd_decode(xbc, decode["state_conv"], dt, zed, 0, decode["state_ssm"],
                                 wts["conv_w"], wts["conv_b"], wts["dt_bias"], wts["a_log"], wts["d_skip"],
                                 wts["ssm_norm_w"], dseq=seq, d_ssm=d_ssm)
        conv_new = jnp.concatenate([decode["state_conv"], xbc.reshape(batch, seq, conv_dim)],
                                   axis=1)[:, -(CONV_WIDTH - 1):]

    x1 = _matmul([(o, wts["w_out_a"]), (y, wts["w_out_b"])], n_out=d_model, res=x, tm=tm, name="out_proj")
    h2 = _rmsnorm(x1, wts["ln2_w"])
    up = _matmul([(h2, wts["w_up"])], n_out=wts["w_up"].shape[1], act="relu2", out_dtype=BF16, tm=tm,
                 name="ffn_up")
    out = _matmul([(up, wts["w_down"])], n_out=d_model, res=x1, tm=tm, tk=2048, name="ffn_down")
    n_ssm_heads = d_ssm // SSM_HEAD_DIM
    return (out, k_f32.reshape(batch, seq, n_kv, LANES), v_f32.reshape(batch, seq, n_kv, ATTN_VHEAD),
            ssm_new.reshape(batch, n_ssm_heads, SSM_HEAD_DIM, SSM_STATE), conv_new)


def kernel(x_prompt, x_sample, cache_k, cache_v, state_ssm, state_conv, page_table, ln1_w, w_in, q_norm_w, k_norm_w, lambda_q1, lambda_k1, lambda_q2, lambda_k2, subln_w, conv_w, conv_b, dt_bias, a_log, d_skip, ssm_norm_w, w_out, ln2_w, w_up, w_down):
    depth = w_in.shape[0]
    bp, seq, d_model = x_prompt.shape
    bs, dseq, _ = x_sample.shape
    page = cache_k.shape[2]
    past_len = page_table.shape[1] * page
    pos_p = jnp.arange(seq, dtype=jnp.int32)
    pos_s = past_len + jnp.arange(dseq, dtype=jnp.int32)
    d_attn = d_model // 2
    d_in = w_in.shape[2]
    n_dt = (d_model - d_attn) // SSM_HEAD_DIM

    yp = x_prompt.reshape(bp * seq, d_model)
    ys = x_sample.reshape(bs * dseq, d_model)
    outs = [[] for _ in range(8)]
    for l in range(depth):
        lam_init = 0.8 - 0.6 * math.exp(-0.3 * l)
        w_in_l = w_in[l]
        wts = dict(
            ln1_w=ln1_w[l], w_in=w_in_l.astype(BF16),
            w_dt=jnp.pad(w_in_l[:, d_in - n_dt:], ((0, 0), (0, LANES - n_dt))).astype(BF16),
            q_norm_w=q_norm_w[l], k_norm_w=k_norm_w[l],
            lambda_q1=lambda_q1[l], lambda_k1=lambda_k1[l], lambda_q2=lambda_q2[l], lambda_k2=lambda_k2[l],
            subln_w=subln_w[l], conv_w=conv_w[l], conv_b=conv_b[l], dt_bias=dt_bias[l], a_log=a_log[l],
            d_skip=d_skip[l], ssm_norm_w=ssm_norm_w[l],
            w_out_a=w_out[l, :d_attn].astype(BF16), w_out_b=w_out[l, d_attn:].astype(BF16),
            ln2_w=ln2_w[l], w_up=w_up[l].astype(BF16), w_down=w_down[l].astype(BF16))
        yp, kp, vp, sp, cp = _layer(yp, batch=bp, seq=seq, pos=pos_p, wts=wts, lam_init=lam_init)
        n_pool = cache_k.shape[1]
        dec = dict(cache_k=cache_k[l].reshape(n_pool, page, -1), cache_v=cache_v[l].reshape(n_pool, page, -1),
                   page_table=page_table, state_conv=state_conv[l], state_ssm=state_ssm[l])
        ys, ks_, vs_, ss_, cs_ = _layer(ys, batch=bs, seq=dseq, pos=pos_s, wts=wts, lam_init=lam_init,
                                        decode=dec)
        for lst, val in zip(outs, (kp, vp, sp, cp, ks_, vs_, ss_, cs_)):
            lst.append(val)
    stacked = [jnp.stack(lst) for lst in outs]
    return (yp.reshape(bp, seq, d_model), ys.reshape(bs, dseq, d_model), *stacked)
```

```python
import functools
import math

import jax
import jax.numpy as jnp
from jax import lax
from jax.experimental import pallas as pl
from jax.experimental.pallas import tpu as pltpu

F32 = jnp.float32
BF16 = jnp.bfloat16

EPS = 1e-6
LANES = 128
SUBLANES = 8
VMEM_LIMIT = 56 * 1024 * 1024

ATTN_VHEAD = 128
ATTN_SUB = 64
ROT_DIM = 16
ROPE_THETA = 500000.0
KV_GROUP = 4
SSM_HEAD_DIM = 64
SSM_STATE = 128
N_SSM_GROUPS = 8
CONV_WIDTH = 4
SSD_CHUNK = 128
NEG_INF = float("-inf")


def _cparams(sem):
    return pltpu.CompilerParams(dimension_semantics=sem, vmem_limit_bytes=VMEM_LIMIT)


def _tile(n, target, mult=1, also=()):
    t = min(n, target)
    while t > 0:
        if n % t == 0 and t % mult == 0 and all(a % t == 0 for a in also):
            return t
        t -= 1
    raise ValueError((n, target, mult, also))


def _rmsnorm_kernel(x_ref, w_ref, o_ref):
    x = x_ref[...]
    ms = jnp.mean(x * x, axis=-1, keepdims=True)
    o_ref[...] = (x * lax.rsqrt(ms + EPS) * w_ref[...]).astype(o_ref.dtype)


def _rmsnorm(x, w, out_dtype=BF16):
    m, d = x.shape
    tr = _tile(m, 256, SUBLANES)
    return pl.pallas_call(
        _rmsnorm_kernel,
        out_shape=jax.ShapeDtypeStruct((m, d), out_dtype),
        grid=(m // tr,),
        in_specs=[pl.BlockSpec((tr, d), lambda i: (i, 0)),
                  pl.BlockSpec((1, d), lambda i: (0, 0))],
        out_specs=pl.BlockSpec((tr, d), lambda i: (i, 0)),
        compiler_params=_cparams(("parallel",)),
        name="rmsnorm",
    )(x, w.reshape(1, d))


def _mm_kernel(*refs, n_pairs, has_res, nk, act):
    pairs = refs[:2 * n_pairs]
    pos = 2 * n_pairs
    res_ref = refs[pos] if has_res else None
    o_ref = refs[pos + int(has_res)]

    def product():
        part = None
        for p in range(n_pairs):
            x = pairs[2 * p][...]
            if x.dtype != BF16:
                x = x.astype(BF16)
            d = jnp.dot(x, pairs[2 * p + 1][...], preferred_element_type=F32)
            part = d if part is None else part + d
        return part

    if nk == 1:
        part = product()
        if act == "relu2":
            part = jnp.square(jnp.maximum(part, 0.0))
        if has_res:
            part = res_ref[...] + part
        o_ref[...] = part.astype(o_ref.dtype)
    else:
        k = pl.program_id(2)

        @pl.when(k == 0)
        def _():
            o_ref[...] = (res_ref[...] + product()) if has_res else product()

        @pl.when(k > 0)
        def _():
            o_ref[...] += product()


def _matmul(pairs, *, n_out, w_col0=0, res=None, act=None, out_dtype=F32,
            tm=1024, tn=1024, tk=None, name="matmul"):
    m, kdim = pairs[0][0].shape
    tm = _tile(m, tm, SUBLANES)
    tn = _tile(n_out, tn, LANES, also=(w_col0,))
    tk = kdim if tk is None else _tile(kdim, tk, LANES)
    nk = kdim // tk
    if nk > 1:
        assert act is None and out_dtype == F32
    c0 = w_col0 // tn
    in_specs, args = [], []
    for x, w in pairs:
        assert x.shape == (m, kdim) and w.shape[0] == kdim
        in_specs += [pl.BlockSpec((tm, tk), lambda i, j, k: (i, k)),
                     pl.BlockSpec((tk, tn), lambda i, j, k: (k, j + c0))]
        args += [x, w]
    if res is not None:
        in_specs.append(pl.BlockSpec((tm, tn), lambda i, j, k: (i, j)))
        args.append(res)
    kern = functools.partial(_mm_kernel, n_pairs=len(pairs), has_res=res is not None, nk=nk, act=act)
    return pl.pallas_call(
        kern,
        out_shape=jax.ShapeDtypeStruct((m, n_out), out_dtype),
        grid=(m // tm, n_out // tn, nk),
        in_specs=in_specs,
        out_specs=pl.BlockSpec((tm, tn), lambda i, j, k: (i, j)),
        compiler_params=_cparams(("parallel", "parallel", "arbitrary")),
        name=name,
    )(*args)


def _norm_rope(x, w, cos, sa, sb, lo):
    x2 = x * x
    s_lo = jnp.sum(jnp.where(lo, x2, 0.0), axis=-1, keepdims=True)
    s_hi = jnp.sum(jnp.where(lo, 0.0, x2), axis=-1, keepdims=True)
    ms = jnp.where(lo, s_lo, s_hi) * (1.0 / ATTN_SUB)
    y = x * lax.rsqrt(ms + EPS) * w
    up = pltpu.roll(y, LANES - ROT_DIM // 2, 1)
    dn = pltpu.roll(y, ROT_DIM // 2, 1)
    return y * cos + up * sa + dn * sb


def _qkv_post_decode_kernel(qkv_ref, cos_ref, sa_ref, sb_ref, qw_ref, kw_ref, q_ref, kf_ref, vf_ref,
                            *, d_q, d_k):
    tr = qkv_ref.shape[0]
    cos, sa, sb = cos_ref[...], sa_ref[...], sb_ref[...]
    lo = lax.broadcasted_iota(jnp.int32, (tr, LANES), 1) < ATTN_SUB
    qw, kw = qw_ref[...], kw_ref[...]
    scale = ATTN_SUB ** -0.5
    for g in range(d_q // LANES):
        sl = slice(g * LANES, (g + 1) * LANES)
        q_ref[:, sl] = _norm_rope(qkv_ref[:, sl], qw, cos, sa, sb, lo) * scale
    for g in range(d_k // LANES):
        sl = slice(g * LANES, (g + 1) * LANES)
        kf_ref[:, sl] = _norm_rope(qkv_ref[:, d_q + g * LANES:d_q + (g + 1) * LANES], kw, cos, sa, sb, lo)
    vf_ref[...] = qkv_ref[:, d_q + d_k:]


def _qkv_post_prompt_kernel(qkv_ref, cos_ref, sa_ref, sb_ref, qw_ref, kw_ref,
                            qt_ref, kf_ref, k0_ref, k1_ref, vf_ref, vt_ref, *, d_q, d_k):
    tr = qkv_ref.shape[0]
    cos, sa, sb = cos_ref[...], sa_ref[...], sb_ref[...]
    lo = lax.broadcasted_iota(jnp.int32, (tr, LANES), 1) < ATTN_SUB
    qw, kw = qw_ref[...], kw_ref[...]
    scale = (ATTN_SUB ** -0.5) * math.log2(math.e)
    for hd in range(d_q // LANES):
        y = _norm_rope(qkv_ref[:, hd * LANES:(hd + 1) * LANES], qw, cos, sa, sb, lo) * scale
        qt_ref[hd] = y.T.astype(BF16)
    for g in range(d_k // LANES):
        y = _norm_rope(qkv_ref[:, d_q + g * LANES:d_q + (g + 1) * LANES], kw, cos, sa, sb, lo)
        sl = slice(g * LANES, (g + 1) * LANES)
        kf_ref[:, sl] = y
        k0_ref[:, sl] = jnp.where(lo, y, 0.0).astype(BF16)
        k1_ref[:, sl] = jnp.where(lo, 0.0, y).astype(BF16)
    v = qkv_ref[:, d_q + d_k:]
    vf_ref[...] = v
    for g in range(v.shape[1] // LANES):
        vt_ref[g] = v[:, g * LANES:(g + 1) * LANES].T.astype(BF16)


def _rope_tables(pos):
    half = ROT_DIM // 2
    inv = jnp.exp(-math.log(ROPE_THETA) * jnp.arange(half, dtype=F32) * 2.0 / ROT_DIM)
    ang = pos.astype(F32)[:, None] * inv[None, :]
    cos, sin = jnp.cos(ang), jnp.sin(ang)
    n = pos.shape[0]
    ones = jnp.ones((n, ATTN_SUB - ROT_DIM), F32)
    zeros8 = jnp.zeros((n, half), F32)
    zeros = jnp.zeros((n, ATTN_SUB - ROT_DIM), F32)
    c = jnp.concatenate([cos, cos, ones], axis=1)
    sa = jnp.concatenate([-sin, zeros8, zeros], axis=1)
    sb = jnp.concatenate([zeros8, sin, zeros], axis=1)
    rep = LANES // ATTN_SUB
    return jnp.tile(c, (1, rep)), jnp.tile(sa, (1, rep)), jnp.tile(sb, (1, rep))


def _qkv_post(qkv, tables, q_norm_w, k_norm_w, *, d_q, d_k, seq_blocks=None):
    m = qkv.shape[0]
    d_v = qkv.shape[1] - d_q - d_k
    n_tab = tables[0].shape[0]
    rep = LANES // ATTN_SUB
    row = lambda i: (i, 0)
    fix = lambda i: (0, 0)
    if seq_blocks is None:
        tr = _tile(m, 256, SUBLANES, also=(n_tab,))
    else:
        tr = n_tab // seq_blocks
    ntb = n_tab // tr
    tab = lambda i: (i % ntb, 0)
    in_specs = [pl.BlockSpec((tr, qkv.shape[1]), row),
                pl.BlockSpec((tr, LANES), tab), pl.BlockSpec((tr, LANES), tab),
                pl.BlockSpec((tr, LANES), tab),
                pl.BlockSpec((1, LANES), fix), pl.BlockSpec((1, LANES), fix)]
    args = (qkv, *tables, jnp.tile(q_norm_w, rep).reshape(1, LANES), jnp.tile(k_norm_w, rep).reshape(1, LANES))
    if seq_blocks is None:
        return pl.pallas_call(
            functools.partial(_qkv_post_decode_kernel, d_q=d_q, d_k=d_k),
            out_shape=(jax.ShapeDtypeStruct((m, d_q), F32), jax.ShapeDtypeStruct((m, d_k), F32),
                       jax.ShapeDtypeStruct((m, d_v), F32)),
            grid=(m // tr,),
            in_specs=in_specs,
            out_specs=(pl.BlockSpec((tr, d_q), row), pl.BlockSpec((tr, d_k), row),
                       pl.BlockSpec((tr, d_v), row)),
            compiler_params=_cparams(("parallel",)),
            name="qkv_post_decode",
        )(*args)
    batch = m // n_tab
    n_heads, n_kv = d_q // LANES, d_v // LANES
    tmap = lambda i: (i // seq_blocks, 0, i % seq_blocks, 0, 0)
    return pl.pallas_call(
        functools.partial(_qkv_post_prompt_kernel, d_q=d_q, d_k=d_k),
        out_shape=(jax.ShapeDtypeStruct((batch, n_heads, seq_blocks, LANES, tr), BF16),
                   jax.ShapeDtypeStruct((m, d_k), F32),
                   jax.ShapeDtypeStruct((m, d_k), BF16), jax.ShapeDtypeStruct((m, d_k), BF16),
                   jax.ShapeDtypeStruct((m, d_v), F32),
                   jax.ShapeDtypeStruct((batch, n_kv, seq_blocks, LANES, tr), BF16)),
        grid=(m // tr,),
        in_specs=in_specs,
        out_specs=(pl.BlockSpec((None, n_heads, None, LANES, tr), tmap),
                   pl.BlockSpec((tr, d_k), row), pl.BlockSpec((tr, d_k), row), pl.BlockSpec((tr, d_k), row),
                   pl.BlockSpec((tr, d_v), row),
                   pl.BlockSpec((None, n_kv, None, LANES, tr), tmap)),
        compiler_params=_cparams(("parallel",)),
        name="qkv_post_prompt",
    )(*args)


def _lambda_value(lam_ref, lam_init):
    l = lam_ref[...]
    d1 = jnp.sum(l[0:1] * l[1:2], axis=-1, keepdims=True)
    d2 = jnp.sum(l[2:3] * l[3:4], axis=-1, keepdims=True)
    return jnp.exp(d1) - jnp.exp(d2) + lam_init


def _subln(o, w, lam_init):
    ms = jnp.mean(o * o, axis=-1, keepdims=True)
    return (o * lax.rsqrt(ms + EPS) * w) * (1.0 - lam_init)


def _dot_nt(a, b):
    return lax.dot_general(a, b, (((1,), (1,)), ((), ())), preferred_element_type=F32)


ONES_ROWS = 16
ATTN_BLOCK = 256


def _attn_prompt_kernel(qt_ref, k0_ref, k1_ref, vt_ref, lam_ref, subw_ref, o_ref, acc_scr,
                        *, blk, lam_init):
    i = pl.program_id(2)
    acc_scr[...] = jnp.zeros(acc_scr.shape, F32)
    ones = jnp.ones((ONES_ROWS, blk), BF16)
    k_refs = (k0_ref, k1_ref)

    def kv_block(j, m_all, diagonal):
        vt = jnp.concatenate([vt_ref[j], ones], axis=0)
        start = pl.multiple_of(j * blk, blk)
        if diagonal:
            visible = (lax.broadcasted_iota(jnp.int32, (blk, blk), 0)
                       <= lax.broadcasted_iota(jnp.int32, (blk, blk), 1))
        kcs = [k_refs[c][pl.ds(start, blk), :] for c in range(2)]
        hcs = [(h, c) for h in range(KV_GROUP) for c in range(2)]
        scores = [jnp.dot(kcs[c], qt_ref[h], preferred_element_type=F32) for h, c in hcs]
        m_rows, alphas, probs = [], [], []
        for idx, s in enumerate(scores):
            if diagonal:
                s = jnp.where(visible, s, NEG_INF)
            m_old = m_all[idx:idx + 1, :]
            m_new = jnp.maximum(m_old, jnp.max(s, axis=0, keepdims=True))
            probs.append(jnp.exp2(s - m_new).astype(BF16))
            alphas.append(jnp.exp2(m_old - m_new))
            m_rows.append(m_new)
        for idx, p in enumerate(probs):
            pv = jnp.dot(vt, p, preferred_element_type=F32)
            acc_scr[idx] = alphas[idx] * acc_scr[idx] + pv
        return jnp.concatenate(m_rows, axis=0)

    m_init = jnp.full((2 * KV_GROUP, blk), NEG_INF, F32)
    m_all = lax.fori_loop(0, i, lambda j, m: kv_block(j, m, False), m_init)
    kv_block(i, m_all, True)

    lam = _lambda_value(lam_ref, lam_init)
    w = subw_ref[...]
    for h in range(KV_GROUP):
        a0, a1 = acc_scr[2 * h], acc_scr[2 * h + 1]
        o0 = a0[:LANES] / a0[LANES:LANES + 1]
        o1 = a1[:LANES] / a1[LANES:LANES + 1]
        o = _subln((o0 - lam * o1).T, w, lam_init)
        o_ref[:, h * LANES:(h + 1) * LANES] = o.astype(o_ref.dtype)


def _attn_prompt(qt, k0, k1, vt, lam_rows, subln_w, *, lam_init):
    batch, n_heads, nb, _, blk = qt.shape
    n_kv = vt.shape[1]
    seq = nb * blk
    gw = KV_GROUP * LANES
    fix = lambda b, g, i: (0, 0)
    kern = functools.partial(_attn_prompt_kernel, blk=blk, lam_init=lam_init)
    return pl.pallas_call(
        kern,
        out_shape=jax.ShapeDtypeStruct((batch * seq, n_heads * LANES), BF16),
        grid=(batch, n_kv, nb),
        in_specs=[pl.BlockSpec((None, KV_GROUP, None, LANES, blk), lambda b, g, i: (b, g, i, 0, 0)),
                  pl.BlockSpec((seq, LANES), lambda b, g, i: (b, g)),
                  pl.BlockSpec((seq, LANES), lambda b, g, i: (b, g)),
                  pl.BlockSpec((None, None, nb, LANES, blk), lambda b, g, i: (b, g, 0, 0, 0)),
                  pl.BlockSpec((4, ATTN_SUB), fix),
                  pl.BlockSpec((1, LANES), fix)],
        out_specs=pl.BlockSpec((blk, gw), lambda b, g, i: (b * nb + i, g)),
        scratch_shapes=[pltpu.VMEM((2 * KV_GROUP, LANES + ONES_ROWS, blk), F32)],
        compiler_params=_cparams(("parallel", "parallel", "parallel")),
        name="attn_prompt",
    )(qt, k0, k1, vt, lam_rows, subln_w.reshape(1, LANES))


def _attn_decode_kernel(pt_ref, q_ref, kn_ref, vn_ref, lam_ref, subw_ref, *rest,
                        pps, n_steps, dseq, n_kv, lam_init):
    k_refs = rest[:pps]
    v_refs = rest[pps:2 * pps]
    o_ref = rest[2 * pps]
    qf_scr, qb_scr, kc_scr, vc_scr, m_scr, l_scr, acc_scr = rest[2 * pps + 1:]
    del pt_ref
    j = pl.program_id(1)
    page = k_refs[0].shape[0] // n_kv
    rows = qf_scr.shape[0]
    blk = 2 * dseq

    @pl.when(j == 0)
    def _():
        qf_scr[...] = jnp.zeros(qf_scr.shape, F32)
        lo = lax.broadcasted_iota(jnp.int32, (dseq, LANES), 1) < ATTN_SUB
        for g in range(n_kv):
            for h in range(KV_GROUP):
                hd = g * KV_GROUP + h
                qh = q_ref[:, hd * LANES:(hd + 1) * LANES]
                r0 = hd * blk
                qf_scr[r0:r0 + dseq, g * LANES:(g + 1) * LANES] = jnp.where(lo, qh, 0.0)
                qf_scr[r0 + dseq:r0 + blk, g * LANES:(g + 1) * LANES] = jnp.where(lo, 0.0, qh)
        qb_scr[...] = qf_scr[...].astype(BF16)
        m_scr[...] = jnp.full(m_scr.shape, NEG_INF, F32)
        l_scr[...] = jnp.zeros(l_scr.shape, F32)
        acc_scr[...] = jnp.zeros(acc_scr.shape, F32)

    def online_update(s, v):
        m_old = m_scr[...]
        m_new = jnp.maximum(m_old, jnp.max(s, axis=-1, keepdims=True))
        p = jnp.exp(s - jnp.tile(m_new, (1, s.shape[1] // LANES)))
        alpha = jnp.exp(m_old - m_new)
        l_scr[...] = alpha * l_scr[...] + jnp.sum(p, axis=-1, keepdims=True)
        pv = jnp.dot(p.astype(BF16), v, preferred_element_type=F32)
        acc_scr[...] = jnp.tile(alpha, (1, n_kv)) * acc_scr[...] + pv
        m_scr[...] = m_new

    @pl.when(j < n_steps)
    def _():
        for t in range(pps):
            for g in range(n_kv):
                head_rows = pl.ds(g, page, stride=n_kv)
                cols = slice(g * LANES, (g + 1) * LANES)
                kc_scr[t * page:(t + 1) * page, cols] = k_refs[t][head_rows, :].astype(BF16)
                vc_scr[t * page:(t + 1) * page, cols] = v_refs[t][head_rows, :].astype(BF16)
        online_update(_dot_nt(qb_scr[...], kc_scr[...]), vc_scr[...])

    @pl.when(j == n_steps)
    def _():
        kpad = jnp.concatenate([kn_ref[...], jnp.zeros((page - dseq, kn_ref.shape[1]), F32)], axis=0)
        vpad = jnp.concatenate([vn_ref[...], jnp.zeros((page - dseq, vn_ref.shape[1]), F32)], axis=0)
        s = _dot_nt(qb_scr[...], kpad.astype(BF16))
        t_idx = lax.broadcasted_iota(jnp.int32, (rows, page), 0) % dseq
        u_idx = lax.broadcasted_iota(jnp.int32, (rows, page), 1)
        s = jnp.where(u_idx <= t_idx, s, NEG_INF)
        online_update(s, vpad.astype(BF16))

        lam = _lambda_value(lam_ref, lam_init)
        w = subw_ref[...]
        for g in range(n_kv):
            for h in range(KV_GROUP):
                hd = g * KV_GROUP + h
                r0 = hd * blk
                cs = slice(g * LANES, (g + 1) * LANES)
                o0 = acc_scr[r0:r0 + dseq, cs] / l_scr[r0:r0 + dseq, :]
                o1 = acc_scr[r0 + dseq:r0 + blk, cs] / l_scr[r0 + dseq:r0 + blk, :]
                o_ref[:, hd * LANES:(hd + 1) * LANES] = _subln(o0 - lam * o1, w, lam_init)


def _attn_decode(q, k_new, v_new, cache_k, cache_v, page_table, lam_rows, subln_w, *, page, dseq, lam_init):
    m, d_q = q.shape
    n_b, n_pages = page_table.shape
    kvw = k_new.shape[1]
    n_kv = kvw // LANES
    pps = _tile(n_pages, 8)
    n_steps = n_pages // pps
    rows = n_kv * KV_GROUP * 2 * dseq

    def page_map(t):
        def f(b, j, pt):
            return (pt[b, jnp.minimum(j, n_steps - 1) * pps + t], 0)
        return f

    tok = lambda b, j, pt: (b, 0)
    fix = lambda b, j, pt: (0, 0)
    kv_specs = lambda: [pl.BlockSpec((page * n_kv, LANES), page_map(t)) for t in range(pps)]
    kern = functools.partial(_attn_decode_kernel, pps=pps, n_steps=n_steps, dseq=dseq, n_kv=n_kv,
                             lam_init=lam_init)
    return pl.pallas_call(
        kern,
        out_shape=jax.ShapeDtypeStruct((m, d_q), F32),
        grid_spec=pltpu.PrefetchScalarGridSpec(
            num_scalar_prefetch=1,
            grid=(n_b, n_steps + 1),
            in_specs=[pl.BlockSpec((dseq, d_q), tok),
                      pl.BlockSpec((dseq, kvw), tok), pl.BlockSpec((dseq, kvw), tok),
                      pl.BlockSpec((4, ATTN_SUB), fix), pl.BlockSpec((1, LANES), fix)]
                     + kv_specs() + kv_specs(),
            out_specs=pl.BlockSpec((dseq, d_q), tok),
            scratch_shapes=[pltpu.VMEM((rows, kvw), F32), pltpu.VMEM((rows, kvw), BF16),
                            pltpu.VMEM((pps * page, kvw), BF16), pltpu.VMEM((pps * page, kvw), BF16),
                            pltpu.VMEM((rows, LANES), F32), pltpu.VMEM((rows, LANES), F32),
                            pltpu.VMEM((rows, kvw), F32)]),
        compiler_params=_cparams(("parallel", "arbitrary")),
        name="attn_decode",
    )(page_table, q, k_new, v_new, lam_rows, subln_w.reshape(1, LANES),
      *([cache_k] * pps), *([cache_v] * pps))


def _silu(x):
    return x * (1.0 / (1.0 + jnp.exp(-x)))


def _conv_taps(xp, w, bias, n_rows):
    acc = None
    for t in range(CONV_WIDTH):
        shift = CONV_WIDTH - 1 - t
        xs = xp if shift == 0 else pltpu.roll(xp, shift, 0)
        term = xs[SUBLANES:] * w[t:t + 1]
        acc = term if acc is None else acc + term
    return _silu(acc + bias)


def _conv_prompt_kernel(x_ref, halo_ref, w_ref, b_ref, o_ref):
    i = pl.program_id(1)
    halo = jnp.where(i > 0, halo_ref[...], 0.0)
    xp = jnp.concatenate([halo, x_ref[...]], axis=0)
    o_ref[...] = _conv_taps(xp, w_ref[...], b_ref[...], x_ref.shape[0])


def _conv_prompt(xbc, conv_w, conv_b, *, batch, seq):
    m, c = xbc.shape
    tr = _tile(seq, 256, SUBLANES)
    tc = _tile(c, 1024, LANES)
    nr = seq // tr
    hb = tr // SUBLANES
    return pl.pallas_call(
        _conv_prompt_kernel,
        out_shape=jax.ShapeDtypeStruct((m, c), F32),
        grid=(batch, nr, c // tc),
        in_specs=[pl.BlockSpec((tr, tc), lambda b, i, j: (b * nr + i, j)),
                  pl.BlockSpec((SUBLANES, tc), lambda b, i, j: (jnp.maximum((b * nr + i) * hb - 1, 0), j)),
                  pl.BlockSpec((CONV_WIDTH, tc), lambda b, i, j: (0, j)),
                  pl.BlockSpec((1, tc), lambda b, i, j: (0, j))],
        out_specs=pl.BlockSpec((tr, tc), lambda b, i, j: (b * nr + i, j)),
        compiler_params=_cparams(("parallel", "parallel", "parallel")),
        name="conv_prompt",
    )(xbc, xbc, conv_w, conv_b.reshape(1, c))


def _cumsum_rows(x):
    n = x.shape[0]
    row = lax.broadcasted_iota(jnp.int32, x.shape, 0)
    s = 1
    while s < n:
        x = x + jnp.where(row >= s, pltpu.roll(x, s, 0), 0.0)
        s *= 2
    return x


def _ssd_chunk(xs, bm, cm, dt_raw, z, state_ref, dtb, alog, dskip, normw, *, n_valid):
    cl, d_ssm = xs.shape
    n_heads = d_ssm // SSM_HEAD_DIM
    hpg = n_heads // N_SSM_GROUPS
    gs = d_ssm // N_SSM_GROUPS
    x_dt = dt_raw + dtb
    dtp = jnp.maximum(x_dt, 0.0) + jnp.log1p(jnp.exp(-jnp.abs(x_dt)))
    if n_valid < cl:
        dtp = jnp.where(lax.broadcasted_iota(jnp.int32, dtp.shape, 0) < n_valid, dtp, 0.0)
    a = -jnp.exp(alog)
    cs = _cumsum_rows(dtp * a)
    cs_t = cs.T
    cs_last = cs[cl - 1:cl, :]
    dec_end = jnp.exp(cs_last - cs)
    dec_in = jnp.exp(cs)
    dec_all = jnp.exp(cs_last)
    tri = lax.broadcasted_iota(jnp.int32, (cl, cl), 0) >= lax.broadcasted_iota(jnp.int32, (cl, cl), 1)
    lo = lax.broadcasted_iota(jnp.int32, (cl, LANES), 1) < SSM_HEAD_DIM

    def pair_cols(arr, h0):
        return jnp.where(lo, arr[:, h0:h0 + 1], arr[:, h0 + 1:h0 + 2])

    y_parts = []
    for g in range(N_SSM_GROUPS):
        bg = bm[:, g * SSM_STATE:(g + 1) * SSM_STATE]
        cg = cm[:, g * SSM_STATE:(g + 1) * SSM_STATE]
        bg16, cg16 = bg.astype(BF16), cg.astype(BF16)
        cb = _dot_nt(cg16, bg16)
        for hp in range(hpg // 2):
            h0 = g * hpg + 2 * hp
            col = slice(h0 * SSM_HEAD_DIM, (h0 + 2) * SSM_HEAD_DIM)
            x_pair = xs[:, col]
            xdt = x_pair * pair_cols(dtp, h0)
            xdt16 = xdt.astype(BF16)
            halves = []
            for e in range(2):
                h = h0 + e
                seg = cs[:, h:h + 1] - cs_t[h:h + 1, :]
                lmat = jnp.exp(jnp.where(tri, seg, NEG_INF))
                scores = (cb * lmat).astype(BF16)
                halves.append(jnp.dot(scores, xdt16, preferred_element_type=F32))
            y_diag = jnp.where(lo, halves[0], halves[1])
            rows = slice(h0 * SSM_HEAD_DIM, (h0 + 2) * SSM_HEAD_DIM)
            st = state_ref[rows, :]
            y_off = _dot_nt(cg16, st.astype(BF16)) * pair_cols(dec_in, h0)
            y_parts.append(y_diag + y_off + pair_cols(dskip, h0) * x_pair)
            xdec = (xdt * pair_cols(dec_end, h0)).astype(BF16)
            new = lax.dot_general(xdec, bg16, (((0,), (0,)), ((), ())), preferred_element_type=F32)
            hsel = lax.broadcasted_iota(jnp.int32, (2 * SSM_HEAD_DIM, SSM_STATE), 0) < SSM_HEAD_DIM
            dec = jnp.where(hsel, dec_all[:, h0:h0 + 1], dec_all[:, h0 + 1:h0 + 2])
            state_ref[rows, :] = dec * st + new
    y = jnp.concatenate(y_parts, axis=1) * _silu(z)
    outs = []
    for g in range(N_SSM_GROUPS):
        yg = y[:, g * gs:(g + 1) * gs]
        ms = jnp.mean(yg * yg, axis=-1, keepdims=True)
        outs.append(yg * lax.rsqrt(ms + EPS) * normw[:, g * gs:(g + 1) * gs])
    return jnp.concatenate(outs, axis=1)


def _ssd_prompt_kernel(xs_ref, b_ref, c_ref, dt_ref, z_ref, dtb_ref, alog_ref, dskip_ref, normw_ref,
                       y_ref, st_ref):
    @pl.when(pl.program_id(1) == 0)
    def _():
        st_ref[...] = jnp.zeros(st_ref.shape, F32)

    y = _ssd_chunk(xs_ref[...], b_ref[...], c_ref[...], dt_ref[...], z_ref[...], st_ref,
                   dtb_ref[...], alog_ref[...], dskip_ref[...], normw_ref[...], n_valid=xs_ref.shape[0])
    y_ref[...] = y.astype(y_ref.dtype)


def _pad_lanes(v):
    return jnp.pad(v, (0, LANES - v.shape[0])).reshape(1, LANES)


def _ssd_prompt(xbc_act, dt, z_src, z_col0, dt_bias, a_log, d_skip, ssm_norm_w, *, batch, seq, d_ssm):
    m = xbc_act.shape[0]
    cl = min(SSD_CHUNK, seq)
    nc = seq // cl
    gn = N_SSM_GROUPS * SSM_STATE
    n_heads = d_ssm // SSM_HEAD_DIM
    row = lambda b, c: (b * nc + c, 0)
    fix = lambda b, c: (0, 0)
    assert d_ssm % gn == 0 and z_col0 % d_ssm == 0
    return pl.pallas_call(
        _ssd_prompt_kernel,
        out_shape=(jax.ShapeDtypeStruct((m, d_ssm), BF16),
                   jax.ShapeDtypeStruct((batch, n_heads * SSM_HEAD_DIM, SSM_STATE), F32)),
        grid=(batch, nc),
        in_specs=[pl.BlockSpec((cl, d_ssm), row),
                  pl.BlockSpec((cl, gn), lambda b, c: (b * nc + c, d_ssm // gn)),
                  pl.BlockSpec((cl, gn), lambda b, c: (b * nc + c, d_ssm // gn + 1)),
                  pl.BlockSpec((cl, LANES), row),
                  pl.BlockSpec((cl, d_ssm), lambda b, c: (b * nc + c, z_col0 // d_ssm)),
                  pl.BlockSpec((1, LANES), fix), pl.BlockSpec((1, LANES), fix),
                  pl.BlockSpec((1, LANES), fix), pl.BlockSpec((1, d_ssm), fix)],
        out_specs=(pl.BlockSpec((cl, d_ssm), row),
                   pl.BlockSpec((None, n_heads * SSM_HEAD_DIM, SSM_STATE), lambda b, c: (b, 0, 0))),
        compiler_params=_cparams(("parallel", "arbitrary")),
        name="ssd_prompt",
    )(xbc_act, xbc_act, xbc_act, dt, z_src, _pad_lanes(dt_bias), _pad_lanes(a_log), _pad_lanes(d_skip),
      ssm_norm_w.reshape(1, d_ssm))


def _ssd_decode_kernel(xbc_ref, prev_ref, dt_ref, z_ref, st0_ref, cw_ref, cb_ref, dtb_ref, alog_ref,
                       dskip_ref, normw_ref, y_ref, st_ref, *, dseq, d_ssm, cl):
    gn = N_SSM_GROUPS * SSM_STATE
    prev = prev_ref[...]
    c = prev.shape[1]
    xp = jnp.concatenate([jnp.zeros((SUBLANES - (CONV_WIDTH - 1), c), F32), prev, xbc_ref[...]], axis=0)
    act = _conv_taps(xp, cw_ref[...], cb_ref[...], dseq)
    pad = lambda v: jnp.concatenate([v, jnp.zeros((cl - dseq, v.shape[1]), F32)], axis=0)
    act = pad(act)
    st_ref[...] = st0_ref[...]
    y = _ssd_chunk(act[:, :d_ssm], act[:, d_ssm:d_ssm + gn], act[:, d_ssm + gn:], pad(dt_ref[...]),
                   pad(z_ref[...]), st_ref, dtb_ref[...], alog_ref[...], dskip_ref[...], normw_ref[...],
                   n_valid=dseq)
    y_ref[...] = y[:dseq]


def _ssd_decode(xbc, conv_prev, dt, z_src, z_col0, state0, conv_w, conv_b, dt_bias, a_log, d_skip,
                ssm_norm_w, *, dseq, d_ssm):
    m, c = xbc.shape
    n_b = m // dseq
    n_heads = d_ssm // SSM_HEAD_DIM
    st_rows = n_heads * SSM_HEAD_DIM
    cl = SSD_CHUNK
    tok = lambda b: (b, 0)
    fix = lambda b: (0, 0)
    st = lambda b: (b, 0, 0)
    assert z_col0 % d_ssm == 0
    kern = functools.partial(_ssd_decode_kernel, dseq=dseq, d_ssm=d_ssm, cl=cl)
    return pl.pallas_call(
        kern,
        out_shape=(jax.ShapeDtypeStruct((m, d_ssm), F32),
                   jax.ShapeDtypeStruct((n_b, st_rows, SSM_STATE), F32)),
        grid=(n_b,),
        in_specs=[pl.BlockSpec((dseq, c), tok),
                  pl.BlockSpec((None, CONV_WIDTH - 1, c), st),
                  pl.BlockSpec((dseq, LANES), tok),
                  pl.BlockSpec((dseq, d_ssm), lambda b: (b, z_col0 // d_ssm)),
                  pl.BlockSpec((None, st_rows, SSM_STATE), st),
                  pl.BlockSpec((CONV_WIDTH, c), fix), pl.BlockSpec((1, c), fix),
                  pl.BlockSpec((1, LANES), fix), pl.BlockSpec((1, LANES), fix),
                  pl.BlockSpec((1, LANES), fix), pl.BlockSpec((1, d_ssm), fix)],
        out_specs=(pl.BlockSpec((dseq, d_ssm), tok),
                   pl.BlockSpec((None, st_rows, SSM_STATE), st)),
        compiler_params=_cparams(("parallel",)),
        name="ssd_decode",
    )(xbc, conv_prev, dt, z_src, state0.reshape(n_b, st_rows, SSM_STATE), conv_w, conv_b.reshape(1, c),
      _pad_lanes(dt_bias), _pad_lanes(a_log), _pad_lanes(d_skip), ssm_norm_w.reshape(1, d_ssm))


def _layer(x, *, batch, seq, pos, wts, lam_init, decode=None):
    m, d_model = x.shape
    d_attn = d_model // 2
    d_ssm = d_model - d_attn
    n_heads = d_attn // ATTN_VHEAD
    n_kv = max(1, n_heads // KV_GROUP)
    d_q, d_k, d_v = n_heads * LANES, n_kv * LANES, n_kv * ATTN_VHEAD
    conv_dim = d_ssm + 2 * N_SSM_GROUPS * SSM_STATE
    tm = 1024 if decode is None else 256
    c_z = d_q + d_k + d_v
    c_x = c_z + d_ssm

    h = _rmsnorm(x, wts["ln1_w"])
    qkv = _matmul([(h, wts["w_in"])], n_out=c_z, w_col0=0, tm=tm, name="in_proj_qkv")
    zed = _matmul([(h, wts["w_in"])], n_out=d_ssm, w_col0=c_z, tm=tm, name="in_proj_z")
    xbc = _matmul([(h, wts["w_in"])], n_out=conv_dim, w_col0=c_x, tm=tm, name="in_proj_xbc")
    dt = _matmul([(h, wts["w_dt"])], n_out=LANES, tm=tm, name="in_proj_dt")

    tables = _rope_tables(pos)
    lam_rows = jnp.stack([wts["lambda_q1"], wts["lambda_k1"], wts["lambda_q2"], wts["lambda_k2"]])

    if decode is None:
        blk = _tile(seq, ATTN_BLOCK, LANES)
        qt, k_f32, k0, k1, v_f32, vt = _qkv_post(qkv, tables, wts["q_norm_w"], wts["k_norm_w"],
                                                 d_q=d_q, d_k=d_k, seq_blocks=seq // blk)
        o = _attn_prompt(qt, k0, k1, vt, lam_rows, wts["subln_w"], lam_init=lam_init)
        act = _conv_prompt(xbc, wts["conv_w"], wts["conv_b"], batch=batch, seq=seq)
        y, ssm_new = _ssd_prompt(act, dt, zed, 0, wts["dt_bias"], wts["a_log"], wts["d_skip"],
                                 wts["ssm_norm_w"], batch=batch, seq=seq, d_ssm=d_ssm)
        conv_new = xbc.reshape(batch, seq, conv_dim)[:, seq - (CONV_WIDTH - 1):]
    else:
        tables = tuple(jnp.tile(t, (batch, 1)) for t in tables)
        q, k_f32, v_f32 = _qkv_post(qkv, tables, wts["q_norm_w"], wts["k_norm_w"], d_q=d_q, d_k=d_k)
        o = _attn_decode(q, k_f32, v_f32, decode["cache_k"], decode["cache_v"], decode["page_table"],
                         lam_rows, wts["subln_w"], page=decode["page"], dseq=seq, lam_init=lam_init)
        y, ssm_new = _ssd_decode(xbc, decode["state_conv"], dt, zed, 0, decode["state_ssm"],
                                 wts["conv_w"], wts["conv_b"], wts["dt_bias"], wts["a_log"], wts["d_skip"],
                                 wts["ssm_norm_w"], dseq=seq, d_ssm=d_ssm)
        conv_new = jnp.concatenate([decode["state_conv"], xbc.reshape(batch, seq, conv_dim)],
                                   axis=1)[:, -(CONV_WIDTH - 1):]

    x1 = _matmul([(o, wts["w_out_a"]), (y, wts["w_out_b"])], n_out=d_model, res=x, tm=tm, name="out_proj")
    h2 = _rmsnorm(x1, wts["ln2_w"])
    up = _matmul([(h2, wts["w_up"])], n_out=wts["w_up"].shape[1], act="relu2", out_dtype=BF16, tm=tm,
                 name="ffn_up")
    out = _matmul([(up, wts["w_down"])], n_out=d_model, res=x1, tm=tm, tk=4096, name="ffn_down")
    n_ssm_heads = d_ssm // SSM_HEAD_DIM
    return (out, k_f32.reshape(batch, seq, n_kv, LANES), v_f32.reshape(batch, seq, n_kv, ATTN_VHEAD),
            ssm_new.reshape(batch, n_ssm_heads, SSM_HEAD_DIM, SSM_STATE), conv_new)


def kernel(x_prompt, x_sample, cache_k, cache_v, state_ssm, state_conv, page_table, ln1_w, w_in, q_norm_w, k_norm_w, lambda_q1, lambda_k1, lambda_q2, lambda_k2, subln_w, conv_w, conv_b, dt_bias, a_log, d_skip, ssm_norm_w, w_out, ln2_w, w_up, w_down):
    depth = w_in.shape[0]
    bp, seq, d_model = x_prompt.shape
    bs, dseq, _ = x_sample.shape
    page = cache_k.shape[2]
    past_len = page_table.shape[1] * page
    pos_p = jnp.arange(seq, dtype=jnp.int32)
    pos_s = past_len + jnp.arange(dseq, dtype=jnp.int32)
    d_attn = d_model // 2
    d_in = w_in.shape[2]
    n_dt = (d_model - d_attn) // SSM_HEAD_DIM

    yp = x_prompt.reshape(bp * seq, d_model)
    ys = x_sample.reshape(bs * dseq, d_model)
    outs = [[] for _ in range(8)]
    for l in range(depth):
        lam_init = 0.8 - 0.6 * math.exp(-0.3 * l)
        w_in_l = w_in[l]
        wts = dict(
            ln1_w=ln1_w[l], w_in=w_in_l.astype(BF16),
            w_dt=jnp.pad(w_in_l[:, d_in - n_dt:], ((0, 0), (0, LANES - n_dt))).astype(BF16),
            q_norm_w=q_norm_w[l], k_norm_w=k_norm_w[l],
            lambda_q1=lambda_q1[l], lambda_k1=lambda_k1[l], lambda_q2=lambda_q2[l], lambda_k2=lambda_k2[l],
            subln_w=subln_w[l], conv_w=conv_w[l], conv_b=conv_b[l], dt_bias=dt_bias[l], a_log=a_log[l],
            d_skip=d_skip[l], ssm_norm_w=ssm_norm_w[l],
            w_out_a=w_out[l, :d_attn].astype(BF16), w_out_b=w_out[l, d_attn:].astype(BF16),
            ln2_w=ln2_w[l], w_up=w_up[l].astype(BF16), w_down=w_down[l].astype(BF16))
        yp, kp, vp, sp, cp = _layer(yp, batch=bp, seq=seq, pos=pos_p, wts=wts, lam_init=lam_init)
        dec = dict(cache_k=cache_k[l].reshape(-1, cache_k.shape[-1]), cache_v=cache_v[l].reshape(-1, cache_v.shape[-1]),
                   page=page, page_table=page_table, state_conv=state_conv[l], state_ssm=state_ssm[l])
        ys, ks_, vs_, ss_, cs_ = _layer(ys, batch=bs, seq=dseq, pos=pos_s, wts=wts, lam_init=lam_init,
                                        decode=dec)
        for lst, val in zip(outs, (kp, vp, sp, cp, ks_, vs_, ss_, cs_)):
            lst.append(val)
    stacked = [jnp.stack(lst) for lst in outs]
    return (yp.reshape(bp, seq, d_model), ys.reshape(bs, dseq, d_model), *stacked)
```

```python
import functools
import math

import jax
import jax.numpy as jnp
from jax import lax
from jax.experimental import pallas as pl
from jax.experimental.pallas import tpu as pltpu

F32 = jnp.float32
BF16 = jnp.bfloat16

EPS = 1e-6
LANES = 128
SUBLANES = 8
VMEM_LIMIT = 56 * 1024 * 1024

ATTN_VHEAD = 128
ATTN_SUB = 64
ROT_DIM = 16
ROPE_THETA = 500000.0
KV_GROUP = 4
SSM_HEAD_DIM = 64
SSM_STATE = 128
N_SSM_GROUPS = 8
CONV_WIDTH = 4
SSD_CHUNK = 128
NEG_INF = float("-inf")


def _cparams(sem):
    return pltpu.CompilerParams(dimension_semantics=sem, vmem_limit_bytes=VMEM_LIMIT)


def _tile(n, target, mult=1, also=()):
    t = min(n, target)
    while t > 0:
        if n % t == 0 and t % mult == 0 and all(a % t == 0 for a in also):
            return t
        t -= 1
    raise ValueError((n, target, mult, also))


def _rmsnorm_kernel(x_ref, w_ref, o_ref):
    x = x_ref[...]
    ms = jnp.mean(x * x, axis=-1, keepdims=True)
    o_ref[...] = (x * lax.rsqrt(ms + EPS) * w_ref[...]).astype(o_ref.dtype)


def _rmsnorm(x, w, out_dtype=BF16):
    m, d = x.shape
    tr = _tile(m, 256, SUBLANES)
    return pl.pallas_call(
        _rmsnorm_kernel,
        out_shape=jax.ShapeDtypeStruct((m, d), out_dtype),
        grid=(m // tr,),
        in_specs=[pl.BlockSpec((tr, d), lambda i: (i, 0)),
                  pl.BlockSpec((1, d), lambda i: (0, 0))],
        out_specs=pl.BlockSpec((tr, d), lambda i: (i, 0)),
        compiler_params=_cparams(("parallel",)),
        name="rmsnorm",
    )(x, w.reshape(1, d))


def _mm_kernel(*refs, n_pairs, has_res, nk, act):
    pairs = refs[:2 * n_pairs]
    pos = 2 * n_pairs
    res_ref = refs[pos] if has_res else None
    o_ref = refs[pos + int(has_res)]

    def product():
        part = None
        for p in range(n_pairs):
            x = pairs[2 * p][...]
            if x.dtype != BF16:
                x = x.astype(BF16)
            d = jnp.dot(x, pairs[2 * p + 1][...], preferred_element_type=F32)
            part = d if part is None else part + d
        return part

    if nk == 1:
        part = product()
        if act == "relu2":
            part = jnp.square(jnp.maximum(part, 0.0))
        if has_res:
            part = res_ref[...] + part
        o_ref[...] = part.astype(o_ref.dtype)
    else:
        k = pl.program_id(2)

        @pl.when(k == 0)
        def _():
            o_ref[...] = (res_ref[...] + product()) if has_res else product()

        @pl.when(k > 0)
        def _():
            o_ref[...] += product()


def _matmul(pairs, *, n_out, w_col0=0, res=None, act=None, out_dtype=F32,
            tm=1024, tn=1024, tk=None, name="matmul"):
    m, kdim = pairs[0][0].shape
    tm = _tile(m, tm, SUBLANES)
    tn = _tile(n_out, tn, LANES, also=(w_col0,))
    tk = kdim if tk is None else _tile(kdim, tk, LANES)
    nk = kdim // tk
    if nk > 1:
        assert act is None and out_dtype == F32
    c0 = w_col0 // tn
    in_specs, args = [], []
    for x, w in pairs:
        assert x.shape == (m, kdim) and w.shape[0] == kdim
        in_specs += [pl.BlockSpec((tm, tk), lambda i, j, k: (i, k)),
                     pl.BlockSpec((tk, tn), lambda i, j, k: (k, j + c0))]
        args += [x, w]
    if res is not None:
        in_specs.append(pl.BlockSpec((tm, tn), lambda i, j, k: (i, j)))
        args.append(res)
    kern = functools.partial(_mm_kernel, n_pairs=len(pairs), has_res=res is not None, nk=nk, act=act)
    return pl.pallas_call(
        kern,
        out_shape=jax.ShapeDtypeStruct((m, n_out), out_dtype),
        grid=(m // tm, n_out // tn, nk),
        in_specs=in_specs,
        out_specs=pl.BlockSpec((tm, tn), lambda i, j, k: (i, j)),
        compiler_params=_cparams(("parallel", "parallel", "arbitrary")),
        name=name,
    )(*args)


def _mmw_kernel(*refs, n_x, has_res, act, tn, c0, nj, w_is_transposed):
    xs = refs[:n_x]
    w_any = refs[n_x]
    xds = refs[n_x + 1:2 * n_x + 1]
    pos = 2 * n_x + 1
    res_ref = resd_ref = None
    if has_res:
        res_ref, resd_ref = refs[pos], refs[pos + 1]
        pos += 2
    o_ref, od_ref, wf_scr, wb_scr, sem = refs[pos:pos + 5]
    j, i = pl.program_id(0), pl.program_id(1)

    def w_copy(jj):
        col = pl.multiple_of((jj + c0) * tn, tn)
        panel = w_any.at[pl.ds(col, tn), :] if w_is_transposed else w_any.at[:, pl.ds(col, tn)]
        return pltpu.make_async_copy(panel, wf_scr, sem)

    def product(x_refs):
        part, k0 = None, 0
        for xr in x_refs:
            x = xr[...]
            if x.dtype != BF16:
                x = x.astype(BF16)
            kp = x.shape[1]
            d = jnp.dot(x, wb_scr[k0:k0 + kp, :], preferred_element_type=F32)
            part = d if part is None else part + d
            k0 += kp
        return part

    def finish(part, r_ref, out_ref):
        if act == "relu2":
            part = jnp.square(jnp.maximum(part, 0.0))
        if r_ref is not None:
            part = r_ref[...] + part
        out_ref[...] = part.astype(out_ref.dtype)

    @pl.when(i == 0)
    def _():
        @pl.when(j == 0)
        def _():
            w_copy(0).start()

        w_copy(j).wait()
        panel = wf_scr[...]
        wb_scr[...] = (panel.T if w_is_transposed else panel).astype(BF16)

        @pl.when(j + 1 < nj)
        def _():
            w_copy(j + 1).start()

        finish(product(xds), resd_ref, od_ref)

    finish(product(xs), res_ref, o_ref)


def _matmul_w32(xs, xds, w, *, n_out, w_col0=0, res=None, resd=None, act=None, out_dtype=F32,
                tm=512, tn=1024, w_is_transposed=False, name="matmul_w32"):
    m, md = xs[0].shape[0], xds[0].shape[0]
    kdim = w.shape[1] if w_is_transposed else w.shape[0]
    assert sum(x.shape[1] for x in xs) == kdim and [x.shape[1] for x in xs] == [x.shape[1] for x in xds]
    tm = _tile(m, tm, SUBLANES)
    tn = _tile(n_out, tn, LANES, also=(w_col0,))
    nj = n_out // tn
    row = lambda j, i: (i, 0)
    one = lambda j, i: (0, 0)
    in_specs = [pl.BlockSpec((tm, x.shape[1]), row) for x in xs]
    in_specs.append(pl.BlockSpec(memory_space=pl.ANY))
    in_specs += [pl.BlockSpec((md, x.shape[1]), one) for x in xds]
    args = [*xs, w, *xds]
    if res is not None:
        in_specs += [pl.BlockSpec((tm, tn), lambda j, i: (i, j)), pl.BlockSpec((md, tn), lambda j, i: (0, j))]
        args += [res, resd]
    kern = functools.partial(_mmw_kernel, n_x=len(xs), has_res=res is not None, act=act, tn=tn,
                             c0=w_col0 // tn, nj=nj, w_is_transposed=w_is_transposed)
    wf_shape = (tn, kdim) if w_is_transposed else (kdim, tn)
    return pl.pallas_call(
        kern,
        out_shape=(jax.ShapeDtypeStruct((m, n_out), out_dtype), jax.ShapeDtypeStruct((md, n_out), out_dtype)),
        grid=(nj, m // tm),
        in_specs=in_specs,
        out_specs=(pl.BlockSpec((tm, tn), lambda j, i: (i, j)), pl.BlockSpec((md, tn), lambda j, i: (0, j))),
        scratch_shapes=[pltpu.VMEM(wf_shape, F32), pltpu.VMEM((kdim, tn), BF16), pltpu.SemaphoreType.DMA],
        compiler_params=_cparams(("arbitrary", "arbitrary")),
        name=name,
    )(*args)


def _dt_proj_kernel(h_ref, w_ref, o_ref):
    n_dt = w_ref.shape[0]
    dt = _dot_nt(h_ref[...], w_ref[...].astype(BF16))
    o_ref[...] = jnp.concatenate([dt, jnp.zeros((dt.shape[0], LANES - n_dt), F32)], axis=1)


def _dt_proj(h, w_t, row0, n_dt):
    m, kdim = h.shape
    tm = _tile(m, 1024, SUBLANES)
    assert row0 % n_dt == 0 and n_dt % SUBLANES == 0
    return pl.pallas_call(
        _dt_proj_kernel,
        out_shape=jax.ShapeDtypeStruct((m, LANES), F32),
        grid=(m // tm,),
        in_specs=[pl.BlockSpec((tm, kdim), lambda i: (i, 0)),
                  pl.BlockSpec((n_dt, kdim), lambda i: (row0 // n_dt, 0))],
        out_specs=pl.BlockSpec((tm, LANES), lambda i: (i, 0)),
        compiler_params=_cparams(("parallel",)),
        name="in_proj_dt",
    )(h, w_t)


def _norm_rope(x, w, cos, sa, sb, lo):
    x2 = x * x
    s_lo = jnp.sum(jnp.where(lo, x2, 0.0), axis=-1, keepdims=True)
    s_hi = jnp.sum(jnp.where(lo, 0.0, x2), axis=-1, keepdims=True)
    ms = jnp.where(lo, s_lo, s_hi) * (1.0 / ATTN_SUB)
    y = x * lax.rsqrt(ms + EPS) * w
    up = pltpu.roll(y, LANES - ROT_DIM // 2, 1)
    dn = pltpu.roll(y, ROT_DIM // 2, 1)
    return y * cos + up * sa + dn * sb


def _qkv_post_decode_kernel(qkv_ref, cos_ref, sa_ref, sb_ref, qw_ref, kw_ref, q_ref, kf_ref, vf_ref,
                            *, d_q, d_k):
    tr = qkv_ref.shape[0]
    cos, sa, sb = cos_ref[...], sa_ref[...], sb_ref[...]
    lo = lax.broadcasted_iota(jnp.int32, (tr, LANES), 1) < ATTN_SUB
    qw, kw = qw_ref[...], kw_ref[...]
    scale = ATTN_SUB ** -0.5
    for g in range(d_q // LANES):
        sl = slice(g * LANES, (g + 1) * LANES)
        q_ref[:, sl] = _norm_rope(qkv_ref[:, sl], qw, cos, sa, sb, lo) * scale
    for g in range(d_k // LANES):
        sl = slice(g * LANES, (g + 1) * LANES)
        kf_ref[:, sl] = _norm_rope(qkv_ref[:, d_q + g * LANES:d_q + (g + 1) * LANES], kw, cos, sa, sb, lo)
    vf_ref[...] = qkv_ref[:, d_q + d_k:]


def _qkv_post_prompt_kernel(qkv_ref, cos_ref, sa_ref, sb_ref, qw_ref, kw_ref,
                            qt_ref, kf_ref, k0_ref, k1_ref, vf_ref, vt_ref, *, d_q, d_k):
    tr = qkv_ref.shape[0]
    cos, sa, sb = cos_ref[...], sa_ref[...], sb_ref[...]
    lo = lax.broadcasted_iota(jnp.int32, (tr, LANES), 1) < ATTN_SUB
    qw, kw = qw_ref[...], kw_ref[...]
    scale = (ATTN_SUB ** -0.5) * math.log2(math.e)
    for hd in range(d_q // LANES):
        y = _norm_rope(qkv_ref[:, hd * LANES:(hd + 1) * LANES], qw, cos, sa, sb, lo) * scale
        qt_ref[hd] = y.T.astype(BF16)
    for g in range(d_k // LANES):
        y = _norm_rope(qkv_ref[:, d_q + g * LANES:d_q + (g + 1) * LANES], kw, cos, sa, sb, lo)
        sl = slice(g * LANES, (g + 1) * LANES)
        kf_ref[:, sl] = y
        k0_ref[:, sl] = jnp.where(lo, y, 0.0).astype(BF16)
        k1_ref[:, sl] = jnp.where(lo, 0.0, y).astype(BF16)
    v = qkv_ref[:, d_q + d_k:]
    vf_ref[...] = v
    for g in range(v.shape[1] // LANES):
        vt_ref[g] = v[:, g * LANES:(g + 1) * LANES].T.astype(BF16)


def _rope_tables(pos):
    half = ROT_DIM // 2
    inv = jnp.exp(-math.log(ROPE_THETA) * jnp.arange(half, dtype=F32) * 2.0 / ROT_DIM)
    ang = pos.astype(F32)[:, None] * inv[None, :]
    cos, sin = jnp.cos(ang), jnp.sin(ang)
    n = pos.shape[0]
    ones = jnp.ones((n, ATTN_SUB - ROT_DIM), F32)
    zeros8 = jnp.zeros((n, half), F32)
    zeros = jnp.zeros((n, ATTN_SUB - ROT_DIM), F32)
    c = jnp.concatenate([cos, cos, ones], axis=1)
    sa = jnp.concatenate([-sin, zeros8, zeros], axis=1)
    sb = jnp.concatenate([zeros8, sin, zeros], axis=1)
    rep = LANES // ATTN_SUB
    return jnp.tile(c, (1, rep)), jnp.tile(sa, (1, rep)), jnp.tile(sb, (1, rep))


def _qkv_post(qkv, tables, q_norm_w, k_norm_w, *, d_q, d_k, d_v, seq_blocks=None):
    m = qkv.shape[0]
    n_tab = tables[0].shape[0]
    rep = LANES // ATTN_SUB
    row = lambda i: (i, 0)
    fix = lambda i: (0, 0)
    if seq_blocks is None:
        tr = _tile(m, 256, SUBLANES, also=(n_tab,))
    else:
        tr = n_tab // seq_blocks
    ntb = n_tab // tr
    tab = lambda i: (i % ntb, 0)
    in_specs = [pl.BlockSpec((tr, d_q + d_k + d_v), row),
                pl.BlockSpec((tr, LANES), tab), pl.BlockSpec((tr, LANES), tab),
                pl.BlockSpec((tr, LANES), tab),
                pl.BlockSpec((1, LANES), fix), pl.BlockSpec((1, LANES), fix)]
    args = (qkv, *tables, jnp.tile(q_norm_w, rep).reshape(1, LANES), jnp.tile(k_norm_w, rep).reshape(1, LANES))
    if seq_blocks is None:
        return pl.pallas_call(
            functools.partial(_qkv_post_decode_kernel, d_q=d_q, d_k=d_k),
            out_shape=(jax.ShapeDtypeStruct((m, d_q), F32), jax.ShapeDtypeStruct((m, d_k), F32),
                       jax.ShapeDtypeStruct((m, d_v), F32)),
            grid=(m // tr,),
            in_specs=in_specs,
            out_specs=(pl.BlockSpec((tr, d_q), row), pl.BlockSpec((tr, d_k), row),
                       pl.BlockSpec((tr, d_v), row)),
            compiler_params=_cparams(("parallel",)),
            name="qkv_post_decode",
        )(*args)
    batch = m // n_tab
    n_heads, n_kv = d_q // LANES, d_v // LANES
    tmap = lambda i: (i // seq_blocks, 0, i % seq_blocks, 0, 0)
    return pl.pallas_call(
        functools.partial(_qkv_post_prompt_kernel, d_q=d_q, d_k=d_k),
        out_shape=(jax.ShapeDtypeStruct((batch, n_heads, seq_blocks, LANES, tr), BF16),
                   jax.ShapeDtypeStruct((m, d_k), F32),
                   jax.ShapeDtypeStruct((m, d_k), BF16), jax.ShapeDtypeStruct((m, d_k), BF16),
                   jax.ShapeDtypeStruct((m, d_v), F32),
                   jax.ShapeDtypeStruct((batch, n_kv, seq_blocks, LANES, tr), BF16)),
        grid=(m // tr,),
        in_specs=in_specs,
        out_specs=(pl.BlockSpec((None, n_heads, None, LANES, tr), tmap),
                   pl.BlockSpec((tr, d_k), row), pl.BlockSpec((tr, d_k), row), pl.BlockSpec((tr, d_k), row),
                   pl.BlockSpec((tr, d_v), row),
                   pl.BlockSpec((None, n_kv, None, LANES, tr), tmap)),
        compiler_params=_cparams(("parallel",)),
        name="qkv_post_prompt",
    )(*args)


def _lambda_value(lam_ref, lam_init):
    l = lam_ref[...]
    d1 = jnp.sum(l[0:1] * l[1:2], axis=-1, keepdims=True)
    d2 = jnp.sum(l[2:3] * l[3:4], axis=-1, keepdims=True)
    return jnp.exp(d1) - jnp.exp(d2) + lam_init


def _subln(o, w, lam_init):
    ms = jnp.mean(o * o, axis=-1, keepdims=True)
    return (o * lax.rsqrt(ms + EPS) * w) * (1.0 - lam_init)


def _dot_nt(a, b):
    return lax.dot_general(a, b, (((1,), (1,)), ((), ())), preferred_element_type=F32)


ONES_ROWS = 16
ATTN_BLOCK = 256


def _attn_prompt_kernel(qt_ref, k0_ref, k1_ref, vt_ref, lam_ref, subw_ref, o_ref, acc_scr,
                        *, blk, lam_init):
    i = pl.program_id(2)
    acc_scr[...] = jnp.zeros(acc_scr.shape, F32)
    ones = jnp.ones((ONES_ROWS, blk), BF16)
    k_refs = (k0_ref, k1_ref)

    def kv_block(j, m_all, diagonal):
        vt = jnp.concatenate([vt_ref[j], ones], axis=0)
        start = pl.multiple_of(j * blk, blk)
        if diagonal:
            visible = (lax.broadcasted_iota(jnp.int32, (blk, blk), 0)
                       <= lax.broadcasted_iota(jnp.int32, (blk, blk), 1))
        kcs = [k_refs[c][pl.ds(start, blk), :] for c in range(2)]
        hcs = [(h, c) for h in range(KV_GROUP) for c in range(2)]
        scores = [jnp.dot(kcs[c], qt_ref[h], preferred_element_type=F32) for h, c in hcs]
        m_rows, alphas, probs = [], [], []
        for idx, s in enumerate(scores):
            if diagonal:
                s = jnp.where(visible, s, NEG_INF)
            m_old = m_all[idx:idx + 1, :]
            m_new = jnp.maximum(m_old, jnp.max(s, axis=0, keepdims=True))
            probs.append(jnp.exp2(s - m_new).astype(BF16))
            alphas.append(jnp.exp2(m_old - m_new))
            m_rows.append(m_new)
        for idx, p in enumerate(probs):
            pv = jnp.dot(vt, p, preferred_element_type=F32)
            acc_scr[idx] = alphas[idx] * acc_scr[idx] + pv
        return jnp.concatenate(m_rows, axis=0)

    m_init = jnp.full((2 * KV_GROUP, blk), NEG_INF, F32)
    m_all = lax.fori_loop(0, i, lambda j, m: kv_block(j, m, False), m_init)
    kv_block(i, m_all, True)

    lam = _lambda_value(lam_ref, lam_init)
    w = subw_ref[...]
    for h in range(KV_GROUP):
        a0, a1 = acc_scr[2 * h], acc_scr[2 * h + 1]
        o0 = a0[:LANES] / a0[LANES:LANES + 1]
        o1 = a1[:LANES] / a1[LANES:LANES + 1]
        o = _subln((o0 - lam * o1).T, w, lam_init)
        o_ref[:, h * LANES:(h + 1) * LANES] = o.astype(o_ref.dtype)


def _attn_prompt(qt, k0, k1, vt, lam_rows, subln_w, *, lam_init):
    batch, n_heads, nb, _, blk = qt.shape
    n_kv = vt.shape[1]
    seq = nb * blk
    gw = KV_GROUP * LANES
    fix = lambda b, g, i: (0, 0)
    kern = functools.partial(_attn_prompt_kernel, blk=blk, lam_init=lam_init)
    return pl.pallas_call(
        kern,
        out_shape=jax.ShapeDtypeStruct((batch * seq, n_heads * LANES), BF16),
        grid=(batch, n_kv, nb),
        in_specs=[pl.BlockSpec((None, KV_GROUP, None, LANES, blk), lambda b, g, i: (b, g, i, 0, 0)),
                  pl.BlockSpec((seq, LANES), lambda b, g, i: (b, g)),
                  pl.BlockSpec((seq, LANES), lambda b, g, i: (b, g)),
                  pl.BlockSpec((None, None, nb, LANES, blk), lambda b, g, i: (b, g, 0, 0, 0)),
                  pl.BlockSpec((4, ATTN_SUB), fix),
                  pl.BlockSpec((1, LANES), fix)],
        out_specs=pl.BlockSpec((blk, gw), lambda b, g, i: (b * nb + i, g)),
        scratch_shapes=[pltpu.VMEM((2 * KV_GROUP, LANES + ONES_ROWS, blk), F32)],
        compiler_params=_cparams(("parallel", "parallel", "parallel")),
        name="attn_prompt",
    )(qt, k0, k1, vt, lam_rows, subln_w.reshape(1, LANES))


def _attn_decode_kernel(pt_ref, q_ref, kn_ref, vn_ref, lam_ref, subw_ref, *rest,
                        pps, n_steps, dseq, n_kv, lam_init):
    k_refs = rest[:pps]
    v_refs = rest[pps:2 * pps]
    o_ref = rest[2 * pps]
    qf_scr, qb_scr, kc_scr, vc_scr, m_scr, l_scr, acc_scr = rest[2 * pps + 1:]
    del pt_ref
    j = pl.program_id(1)
    page = k_refs[0].shape[0] // n_kv
    rows = qf_scr.shape[0]
    blk = 2 * dseq

    @pl.when(j == 0)
    def _():
        qf_scr[...] = jnp.zeros(qf_scr.shape, F32)
        lo = lax.broadcasted_iota(jnp.int32, (dseq, LANES), 1) < ATTN_SUB
        for g in range(n_kv):
            for h in range(KV_GROUP):
                hd = g * KV_GROUP + h
                qh = q_ref[:, hd * LANES:(hd + 1) * LANES]
                r0 = hd * blk
                qf_scr[r0:r0 + dseq, g * LANES:(g + 1) * LANES] = jnp.where(lo, qh, 0.0)
                qf_scr[r0 + dseq:r0 + blk, g * LANES:(g + 1) * LANES] = jnp.where(lo, 0.0, qh)
        qb_scr[...] = qf_scr[...].astype(BF16)
        m_scr[...] = jnp.full(m_scr.shape, NEG_INF, F32)
        l_scr[...] = jnp.zeros(l_scr.shape, F32)
        acc_scr[...] = jnp.zeros(acc_scr.shape, F32)

    def online_update(s, v):
        m_old = m_scr[...]
        m_new = jnp.maximum(m_old, jnp.max(s, axis=-1, keepdims=True))
        p = jnp.exp(s - jnp.tile(m_new, (1, s.shape[1] // LANES)))
        alpha = jnp.exp(m_old - m_new)
        l_scr[...] = alpha * l_scr[...] + jnp.sum(p, axis=-1, keepdims=True)
        pv = jnp.dot(p.astype(BF16), v, preferred_element_type=F32)
        acc_scr[...] = jnp.tile(alpha, (1, n_kv)) * acc_scr[...] + pv
        m_scr[...] = m_new

    @pl.when(j < n_steps)
    def _():
        for t in range(pps):
            for g in range(n_kv):
                head_rows = pl.ds(g, page, stride=n_kv)
                cols = slice(g * LANES, (g + 1) * LANES)
                kc_scr[t * page:(t + 1) * page, cols] = k_refs[t][head_rows, :].astype(BF16)
                vc_scr[t * page:(t + 1) * page, cols] = v_refs[t][head_rows, :].astype(BF16)
        online_update(_dot_nt(qb_scr[...], kc_scr[...]), vc_scr[...])

    @pl.when(j == n_steps)
    def _():
        kpad = jnp.concatenate([kn_ref[...], jnp.zeros((page - dseq, kn_ref.shape[1]), F32)], axis=0)
        vpad = jnp.concatenate([vn_ref[...], jnp.zeros((page - dseq, vn_ref.shape[1]), F32)], axis=0)
        s = _dot_nt(qb_scr[...], kpad.astype(BF16))
        t_idx = lax.broadcasted_iota(jnp.int32, (rows, page), 0) % dseq
        u_idx = lax.broadcasted_iota(jnp.int32, (rows, page), 1)
        s = jnp.where(u_idx <= t_idx, s, NEG_INF)
        online_update(s, vpad.astype(BF16))

        lam = _lambda_value(lam_ref, lam_init)
        w = subw_ref[...]
        for g in range(n_kv):
            for h in range(KV_GROUP):
                hd = g * KV_GROUP + h
                r0 = hd * blk
                cs = slice(g * LANES, (g + 1) * LANES)
                o0 = acc_scr[r0:r0 + dseq, cs] / l_scr[r0:r0 + dseq, :]
                o1 = acc_scr[r0 + dseq:r0 + blk, cs] / l_scr[r0 + dseq:r0 + blk, :]
                o_ref[:, hd * LANES:(hd + 1) * LANES] = _subln(o0 - lam * o1, w, lam_init)


def _attn_decode(q, k_new, v_new, cache_k, cache_v, page_table, lam_rows, subln_w, *, page, dseq, lam_init):
    m, d_q = q.shape
    n_b, n_pages = page_table.shape
    kvw = k_new.shape[1]
    n_kv = kvw // LANES
    pps = _tile(n_pages, 8)
    n_steps = n_pages // pps
    rows = n_kv * KV_GROUP * 2 * dseq

    def page_map(t):
        def f(b, j, pt):
            return (pt[b, jnp.minimum(j, n_steps - 1) * pps + t], 0)
        return f

    tok = lambda b, j, pt: (b, 0)
    fix = lambda b, j, pt: (0, 0)
    kv_specs = lambda: [pl.BlockSpec((page * n_kv, LANES), page_map(t)) for t in range(pps)]
    kern = functools.partial(_attn_decode_kernel, pps=pps, n_steps=n_steps, dseq=dseq, n_kv=n_kv,
                             lam_init=lam_init)
    return pl.pallas_call(
        kern,
        out_shape=jax.ShapeDtypeStruct((m, d_q), F32),
        grid_spec=pltpu.PrefetchScalarGridSpec(
            num_scalar_prefetch=1,
            grid=(n_b, n_steps + 1),
            in_specs=[pl.BlockSpec((dseq, d_q), tok),
                      pl.BlockSpec((dseq, kvw), tok), pl.BlockSpec((dseq, kvw), tok),
                      pl.BlockSpec((4, ATTN_SUB), fix), pl.BlockSpec((1, LANES), fix)]
                     + kv_specs() + kv_specs(),
            out_specs=pl.BlockSpec((dseq, d_q), tok),
            scratch_shapes=[pltpu.VMEM((rows, kvw), F32), pltpu.VMEM((rows, kvw), BF16),
                            pltpu.VMEM((pps * page, kvw), BF16), pltpu.VMEM((pps * page, kvw), BF16),
                            pltpu.VMEM((rows, LANES), F32), pltpu.VMEM((rows, LANES), F32),
                            pltpu.VMEM((rows, kvw), F32)]),
        compiler_params=_cparams(("parallel", "arbitrary")),
        name="attn_decode",
    )(page_table, q, k_new, v_new, lam_rows, subln_w.reshape(1, LANES),
      *([cache_k] * pps), *([cache_v] * pps))


def _silu(x):
    return x * (1.0 / (1.0 + jnp.exp(-x)))


def _conv_taps(xp, w, bias, n_rows):
    acc = None
    for t in range(CONV_WIDTH):
        shift = CONV_WIDTH - 1 - t
        xs = xp if shift == 0 else pltpu.roll(xp, shift, 0)
        term = xs[SUBLANES:] * w[t:t + 1]
        acc = term if acc is None else acc + term
    return _silu(acc + bias)


def _conv_prompt_kernel(x_ref, halo_ref, w_ref, b_ref, o_ref):
    i = pl.program_id(1)
    halo = jnp.where(i > 0, halo_ref[...], 0.0)
    xp = jnp.concatenate([halo, x_ref[...]], axis=0)
    o_ref[...] = _conv_taps(xp, w_ref[...], b_ref[...], x_ref.shape[0])


def _conv_prompt(src, col0, conv_w, conv_b, *, batch, seq):
    m = src.shape[0]
    c = conv_w.shape[1]
    tr = _tile(seq, 256, SUBLANES)
    tc = _tile(c, 1024, LANES, also=(col0,))
    nr = seq // tr
    hb = tr // SUBLANES
    cb0 = col0 // tc
    return pl.pallas_call(
        _conv_prompt_kernel,
        out_shape=jax.ShapeDtypeStruct((m, c), F32),
        grid=(batch, nr, c // tc),
        in_specs=[pl.BlockSpec((tr, tc), lambda b, i, j: (b * nr + i, j + cb0)),
                  pl.BlockSpec((SUBLANES, tc),
                               lambda b, i, j: (jnp.maximum((b * nr + i) * hb - 1, 0), j + cb0)),
                  pl.BlockSpec((CONV_WIDTH, tc), lambda b, i, j: (0, j)),
                  pl.BlockSpec((1, tc), lambda b, i, j: (0, j))],
        out_specs=pl.BlockSpec((tr, tc), lambda b, i, j: (b * nr + i, j)),
        compiler_params=_cparams(("parallel", "parallel", "parallel")),
        name="conv_prompt",
    )(src, src, conv_w, conv_b.reshape(1, c))


def _cumsum_rows(x):
    n = x.shape[0]
    row = lax.broadcasted_iota(jnp.int32, x.shape, 0)
    s = 1
    while s < n:
        x = x + jnp.where(row >= s, pltpu.roll(x, s, 0), 0.0)
        s *= 2
    return x


def _ssd_chunk(xs, bm, cm, dt_raw, z, state_ref, dtb, alog, dskip, normw, *, n_valid):
    cl, d_ssm = xs.shape
    n_heads = d_ssm // SSM_HEAD_DIM
    hpg = n_heads // N_SSM_GROUPS
    gs = d_ssm // N_SSM_GROUPS
    x_dt = dt_raw + dtb
    dtp = jnp.maximum(x_dt, 0.0) + jnp.log1p(jnp.exp(-jnp.abs(x_dt)))
    if n_valid < cl:
        dtp = jnp.where(lax.broadcasted_iota(jnp.int32, dtp.shape, 0) < n_valid, dtp, 0.0)
    a = -jnp.exp(alog)
    cs = _cumsum_rows(dtp * a)
    cs_t = cs.T
    cs_last = cs[cl - 1:cl, :]
    dec_all = jnp.exp(cs_last)
    tri = lax.broadcasted_iota(jnp.int32, (cl, cl), 0) >= lax.broadcasted_iota(jnp.int32, (cl, cl), 1)
    lo = lax.broadcasted_iota(jnp.int32, (cl, LANES), 1) < SSM_HEAD_DIM
    lo_row = lo[0:1]
    hsel = lax.broadcasted_iota(jnp.int32, (2 * SSM_HEAD_DIM, SSM_STATE), 0) < SSM_HEAD_DIM
    n_pairs = n_heads // 2
    group_of = lambda pr: (2 * pr) // hpg

    b16 = [bm[:, g * SSM_STATE:(g + 1) * SSM_STATE].astype(BF16) for g in range(N_SSM_GROUPS)]
    c16 = [cm[:, g * SSM_STATE:(g + 1) * SSM_STATE].astype(BF16) for g in range(N_SSM_GROUPS)]
    cb = [_dot_nt(c16[g], b16[g]) for g in range(N_SSM_GROUPS)]
    states = [state_ref[pr * LANES:(pr + 1) * LANES, :] for pr in range(n_pairs)]
    y_off_raw = [_dot_nt(c16[group_of(pr)], states[pr].astype(BF16)) for pr in range(n_pairs)]

    x_pairs, xdt16, xdec16, scores, dec_in = [], [], [], [], []
    for pr in range(n_pairs):
        h0 = 2 * pr
        full = [jnp.broadcast_to(cs[:, h:h + 1], (cl, LANES)) for h in (h0, h0 + 1)]
        cs_pair = jnp.where(lo, full[0], full[1])
        dt_pair = jnp.where(lo, dtp[:, h0:h0 + 1], dtp[:, h0 + 1:h0 + 2])
        last_pair = jnp.where(lo_row, cs_last[:, h0:h0 + 1], cs_last[:, h0 + 1:h0 + 2])
        x_pair = xs[:, pr * LANES:(pr + 1) * LANES]
        xdt = x_pair * dt_pair
        x_pairs.append(x_pair)
        xdt16.append(xdt.astype(BF16))
        xdec16.append((xdt * jnp.exp(last_pair - cs_pair)).astype(BF16))
        dec_in.append(jnp.exp(cs_pair))
        for e in range(2):
            seg = full[e] - cs_t[h0 + e:h0 + e + 1, :]
            lmat = jnp.exp(jnp.where(tri, seg, NEG_INF))
            scores.append((cb[group_of(pr)] * lmat).astype(BF16))

    y_parts = []
    for pr in range(n_pairs):
        h0 = 2 * pr
        halves = [jnp.dot(scores[h0 + e], xdt16[pr], preferred_element_type=F32) for e in range(2)]
        y_diag = jnp.where(lo, halves[0], halves[1])
        skip = jnp.where(lo_row, dskip[:, h0:h0 + 1], dskip[:, h0 + 1:h0 + 2])
        y_parts.append(y_diag + y_off_raw[pr] * dec_in[pr] + skip * x_pairs[pr])
        new = lax.dot_general(xdec16[pr], b16[group_of(pr)], (((0,), (0,)), ((), ())),
                              preferred_element_type=F32)
        dec = jnp.where(hsel, dec_all[:, h0:h0 + 1], dec_all[:, h0 + 1:h0 + 2])
        state_ref[pr * LANES:(pr + 1) * LANES, :] = dec * states[pr] + new
    y = jnp.concatenate(y_parts, axis=1) * _silu(z)
    outs = []
    for g in range(N_SSM_GROUPS):
        yg = y[:, g * gs:(g + 1) * gs]
        ms = jnp.mean(yg * yg, axis=-1, keepdims=True)
        outs.append(yg * lax.rsqrt(ms + EPS) * normw[:, g * gs:(g + 1) * gs])
    return jnp.concatenate(outs, axis=1)


def _ssd_prompt_kernel(xs_ref, b_ref, c_ref, dt_ref, *rest, n_z):
    z_refs = rest[:n_z]
    dtb_ref, alog_ref, dskip_ref, normw_ref, y_ref, st_ref = rest[n_z:]

    @pl.when(pl.program_id(1) == 0)
    def _():
        st_ref[...] = jnp.zeros(st_ref.shape, F32)

    z = jnp.concatenate([r[...] for r in z_refs], axis=1)
    y = _ssd_chunk(xs_ref[...], b_ref[...], c_ref[...], dt_ref[...], z, st_ref,
                   dtb_ref[...], alog_ref[...], dskip_ref[...], normw_ref[...], n_valid=xs_ref.shape[0])
    y_ref[...] = y.astype(y_ref.dtype)


def _pad_lanes(v):
    return jnp.pad(v, (0, LANES - v.shape[0])).reshape(1, LANES)


def _ssd_prompt(xbc_act, dt, z_src, z_col0, dt_bias, a_log, d_skip, ssm_norm_w, *, batch, seq, d_ssm):
    m = xbc_act.shape[0]
    cl = min(SSD_CHUNK, seq)
    nc = seq // cl
    gn = N_SSM_GROUPS * SSM_STATE
    n_heads = d_ssm // SSM_HEAD_DIM
    row = lambda b, c: (b * nc + c, 0)
    fix = lambda b, c: (0, 0)
    assert d_ssm % gn == 0
    zw = _tile(d_ssm, d_ssm, LANES, also=(z_col0,))
    n_z = d_ssm // zw

    def z_map(t):
        return lambda b, c: (b * nc + c, z_col0 // zw + t)

    return pl.pallas_call(
        functools.partial(_ssd_prompt_kernel, n_z=n_z),
        out_shape=(jax.ShapeDtypeStruct((m, d_ssm), BF16),
                   jax.ShapeDtypeStruct((batch, n_heads * SSM_HEAD_DIM, SSM_STATE), F32)),
        grid=(batch, nc),
        in_specs=[pl.BlockSpec((cl, d_ssm), row),
                  pl.BlockSpec((cl, gn), lambda b, c: (b * nc + c, d_ssm // gn)),
                  pl.BlockSpec((cl, gn), lambda b, c: (b * nc + c, d_ssm // gn + 1)),
                  pl.BlockSpec((cl, LANES), row)]
                 + [pl.BlockSpec((cl, zw), z_map(t)) for t in range(n_z)]
                 + [pl.BlockSpec((1, LANES), fix), pl.BlockSpec((1, LANES), fix),
                    pl.BlockSpec((1, LANES), fix), pl.BlockSpec((1, d_ssm), fix)],
        out_specs=(pl.BlockSpec((cl, d_ssm), row),
                   pl.BlockSpec((None, n_heads * SSM_HEAD_DIM, SSM_STATE), lambda b, c: (b, 0, 0))),
        compiler_params=_cparams(("parallel", "arbitrary")),
        name="ssd_prompt",
    )(xbc_act, xbc_act, xbc_act, dt, *([z_src] * n_z), _pad_lanes(dt_bias), _pad_lanes(a_log),
      _pad_lanes(d_skip), ssm_norm_w.reshape(1, d_ssm))


def _ssd_decode_kernel(xbc_ref, prev_ref, dt_ref, z_ref, st0_ref, cw_ref, cb_ref, dtb_ref, alog_ref,
                       dskip_ref, normw_ref, y_ref, st_ref, *, dseq, d_ssm, cl):
    gn = N_SSM_GROUPS * SSM_STATE
    prev = prev_ref[...]
    c = prev.shape[1]
    xp = jnp.concatenate([jnp.zeros((SUBLANES - (CONV_WIDTH - 1), c), F32), prev, xbc_ref[...]], axis=0)
    act = _conv_taps(xp, cw_ref[...], cb_ref[...], dseq)
    pad = lambda v: jnp.concatenate([v, jnp.zeros((cl - dseq, v.shape[1]), F32)], axis=0)
    act = pad(act)
    st_ref[...] = st0_ref[...]
    y = _ssd_chunk(act[:, :d_ssm], act[:, d_ssm:d_ssm + gn], act[:, d_ssm + gn:], pad(dt_ref[...]),
                   pad(z_ref[...]), st_ref, dtb_ref[...], alog_ref[...], dskip_ref[...], normw_ref[...],
                   n_valid=dseq)
    y_ref[...] = y[:dseq]


def _ssd_decode(xbc, conv_prev, dt, z_src, z_col0, state0, conv_w, conv_b, dt_bias, a_log, d_skip,
                ssm_norm_w, *, dseq, d_ssm):
    m, c = xbc.shape
    n_b = m // dseq
    n_heads = d_ssm // SSM_HEAD_DIM
    st_rows = n_heads * SSM_HEAD_DIM
    cl = SSD_CHUNK
    tok = lambda b: (b, 0)
    fix = lambda b: (0, 0)
    st = lambda b: (b, 0, 0)
    assert z_col0 % d_ssm == 0
    kern = functools.partial(_ssd_decode_kernel, dseq=dseq, d_ssm=d_ssm, cl=cl)
    return pl.pallas_call(
        kern,
        out_shape=(jax.ShapeDtypeStruct((m, d_ssm), F32),
                   jax.ShapeDtypeStruct((n_b, st_rows, SSM_STATE), F32)),
        grid=(n_b,),
        in_specs=[pl.BlockSpec((dseq, c), tok),
                  pl.BlockSpec((None, CONV_WIDTH - 1, c), st),
                  pl.BlockSpec((dseq, LANES), tok),
                  pl.BlockSpec((dseq, d_ssm), lambda b: (b, z_col0 // d_ssm)),
                  pl.BlockSpec((None, st_rows, SSM_STATE), st),
                  pl.BlockSpec((CONV_WIDTH, c), fix), pl.BlockSpec((1, c), fix),
                  pl.BlockSpec((1, LANES), fix), pl.BlockSpec((1, LANES), fix),
                  pl.BlockSpec((1, LANES), fix), pl.BlockSpec((1, d_ssm), fix)],
        out_specs=(pl.BlockSpec((dseq, d_ssm), tok),
                   pl.BlockSpec((None, st_rows, SSM_STATE), st)),
        compiler_params=_cparams(("parallel",)),
        name="ssd_decode",
    )(xbc, conv_prev, dt, z_src, state0.reshape(n_b, st_rows, SSM_STATE), conv_w, conv_b.reshape(1, c),
      _pad_lanes(dt_bias), _pad_lanes(a_log), _pad_lanes(d_skip), ssm_norm_w.reshape(1, d_ssm))


def _layer(xp, xd, *, bp, seq, bd, dseq, pos_p, pos_d, wts, dec, lam_init):
    d_model = xp.shape[1]
    d_attn = d_model // 2
    d_ssm = d_model - d_attn
    n_heads = d_attn // ATTN_VHEAD
    n_kv = max(1, n_heads // KV_GROUP)
    d_q, d_k, d_v = n_heads * LANES, n_kv * LANES, n_kv * ATTN_VHEAD
    conv_dim = d_ssm + 2 * N_SSM_GROUPS * SSM_STATE
    c_z = d_q + d_k + d_v
    c_x = c_z + d_ssm
    c_dt = c_x + conv_dim
    n_ssm_heads = d_ssm // SSM_HEAD_DIM

    hp, hd = _rmsnorm(xp, wts["ln1_w"]), _rmsnorm(xd, wts["ln1_w"])
    proj_p, proj_d = _matmul_w32([hp], [hd], wts["w_in_t"], n_out=c_dt, w_is_transposed=True,
                                 name="in_proj")
    dt_p = _dt_proj(hp, wts["w_in_t"], c_dt, n_ssm_heads)
    dt_d = _dt_proj(hd, wts["w_in_t"], c_dt, n_ssm_heads)

    tab_p = _rope_tables(pos_p)
    tab_d = tuple(jnp.tile(t, (bd, 1)) for t in _rope_tables(pos_d))
    lam_rows = jnp.stack([wts["lambda_q1"], wts["lambda_k1"], wts["lambda_q2"], wts["lambda_k2"]])
    qk = dict(d_q=d_q, d_k=d_k, d_v=d_v)

    blk = _tile(seq, ATTN_BLOCK, LANES)
    qt, kp_f32, k0, k1, vp_f32, vt = _qkv_post(proj_p, tab_p, wts["q_norm_w"], wts["k_norm_w"],
                                               seq_blocks=seq // blk, **qk)
    o_p = _attn_prompt(qt, k0, k1, vt, lam_rows, wts["subln_w"], lam_init=lam_init)
    act = _conv_prompt(proj_p, c_x, wts["conv_w"], wts["conv_b"], batch=bp, seq=seq)
    y_p, ssm_p = _ssd_prompt(act, dt_p, proj_p, c_z, wts["dt_bias"], wts["a_log"], wts["d_skip"],
                             wts["ssm_norm_w"], batch=bp, seq=seq, d_ssm=d_ssm)
    conv_p = proj_p.reshape(bp, seq, c_dt)[:, seq - (CONV_WIDTH - 1):, c_x:]

    q_d, kd_f32, vd_f32 = _qkv_post(proj_d, tab_d, wts["q_norm_w"], wts["k_norm_w"], **qk)
    o_d = _attn_decode(q_d, kd_f32, vd_f32, dec["cache_k"], dec["cache_v"], dec["page_table"],
                       lam_rows, wts["subln_w"], page=dec["page"], dseq=dseq, lam_init=lam_init)
    xbc_d = proj_d[:, c_x:]
    y_d, ssm_d = _ssd_decode(xbc_d, dec["state_conv"], dt_d, proj_d[:, c_z:c_x], 0, dec["state_ssm"],
                             wts["conv_w"], wts["conv_b"], wts["dt_bias"], wts["a_log"], wts["d_skip"],
                             wts["ssm_norm_w"], dseq=dseq, d_ssm=d_ssm)
    conv_d = jnp.concatenate([dec["state_conv"], xbc_d.reshape(bd, dseq, conv_dim)],
                             axis=1)[:, -(CONV_WIDTH - 1):]

    x1_p, x1_d = _matmul_w32([o_p, y_p], [o_d, y_d], wts["w_out"], n_out=d_model, res=xp, resd=xd,
                             name="out_proj")
    h2_p, h2_d = _rmsnorm(x1_p, wts["ln2_w"]), _rmsnorm(x1_d, wts["ln2_w"])
    up_p, up_d = _matmul_w32([h2_p], [h2_d], wts["w_up"], n_out=wts["w_up"].shape[1], act="relu2",
                             out_dtype=BF16, name="ffn_up")
    out_p = _matmul([(up_p, wts["w_down"])], n_out=d_model, res=x1_p, tk=4096, name="ffn_down")
    out_d = _matmul([(up_d, wts["w_down"])], n_out=d_model, res=x1_d, tk=4096, name="ffn_down")

    def heads(a, b, s, w):
        return a.reshape(b, s, n_kv, w)

    state = lambda a, b: a.reshape(b, n_ssm_heads, SSM_HEAD_DIM, SSM_STATE)
    return (out_p, out_d,
            (heads(kp_f32, bp, seq, LANES), heads(vp_f32, bp, seq, ATTN_VHEAD), state(ssm_p, bp), conv_p,
             heads(kd_f32, bd, dseq, LANES), heads(vd_f32, bd, dseq, ATTN_VHEAD), state(ssm_d, bd), conv_d))


def kernel(x_prompt, x_sample, cache_k, cache_v, state_ssm, state_conv, page_table, ln1_w, w_in, q_norm_w, k_norm_w, lambda_q1, lambda_k1, lambda_q2, lambda_k2, subln_w, conv_w, conv_b, dt_bias, a_log, d_skip, ssm_norm_w, w_out, ln2_w, w_up, w_down):
    depth = w_in.shape[0]
    bp, seq, d_model = x_prompt.shape
    bs, dseq, _ = x_sample.shape
    page = cache_k.shape[2]
    past_len = page_table.shape[1] * page
    pos_p = jnp.arange(seq, dtype=jnp.int32)
    pos_s = past_len + jnp.arange(dseq, dtype=jnp.int32)

    yp = x_prompt.reshape(bp * seq, d_model)
    ys = x_sample.reshape(bs * dseq, d_model)
    outs = [[] for _ in range(8)]
    for l in range(depth):
        lam_init = 0.8 - 0.6 * math.exp(-0.3 * l)
        w_in_t = w_in[l].T
        wts = dict(
            ln1_w=ln1_w[l], w_in_t=w_in_t,
            q_norm_w=q_norm_w[l], k_norm_w=k_norm_w[l],
            lambda_q1=lambda_q1[l], lambda_k1=lambda_k1[l], lambda_q2=lambda_q2[l], lambda_k2=lambda_k2[l],
            subln_w=subln_w[l], conv_w=conv_w[l], conv_b=conv_b[l], dt_bias=dt_bias[l], a_log=a_log[l],
            d_skip=d_skip[l], ssm_norm_w=ssm_norm_w[l], w_out=w_out[l],
            ln2_w=ln2_w[l], w_up=w_up[l], w_down=w_down[l].astype(BF16))
        dec = dict(cache_k=cache_k[l].reshape(-1, cache_k.shape[-1]), cache_v=cache_v[l].reshape(-1, cache_v.shape[-1]),
                   page=page, page_table=page_table, state_conv=state_conv[l], state_ssm=state_ssm[l])
        yp, ys, layer_outs = _layer(yp, ys, bp=bp, seq=seq, bd=bs, dseq=dseq, pos_p=pos_p, pos_d=pos_s,
                                    wts=wts, dec=dec, lam_init=lam_init)
        for lst, val in zip(outs, layer_outs):
            lst.append(val)
    stacked = [jnp.stack(lst) for lst in outs]
    return (yp.reshape(bp, seq, d_model), ys.reshape(bs, dseq, d_model), *stacked)
```

```python
import functools
import math

import jax
import jax.numpy as jnp
from jax import lax
from jax.experimental import pallas as pl
from jax.experimental.pallas import tpu as pltpu

F32 = jnp.float32
BF16 = jnp.bfloat16

EPS = 1e-6
LANES = 128
SUBLANES = 8
VMEM_LIMIT = 56 * 1024 * 1024

ATTN_VHEAD = 128
ATTN_SUB = 64
ROT_DIM = 16
ROPE_THETA = 500000.0
KV_GROUP = 4
SSM_HEAD_DIM = 64
SSM_STATE = 128
N_SSM_GROUPS = 8
CONV_WIDTH = 4
SSD_CHUNK = 128
FFN_DOWN_CHUNK = 4096
NEG_INF = float("-inf")


def _cparams(sem):
    return pltpu.CompilerParams(dimension_semantics=sem, vmem_limit_bytes=VMEM_LIMIT)


def _tile(n, target, mult=1, also=()):
    t = min(n, target)
    while t > 0:
        if n % t == 0 and t % mult == 0 and all(a % t == 0 for a in also):
            return t
        t -= 1
    raise ValueError((n, target, mult, also))


def _rmsnorm_kernel(x_ref, w_ref, o_ref):
    x = x_ref[...]
    ms = jnp.mean(x * x, axis=-1, keepdims=True)
    o_ref[...] = (x * lax.rsqrt(ms + EPS) * w_ref[...]).astype(o_ref.dtype)


def _rmsnorm(x, w, out_dtype=BF16):
    m, d = x.shape
    tr = _tile(m, 256, SUBLANES)
    return pl.pallas_call(
        _rmsnorm_kernel,
        out_shape=jax.ShapeDtypeStruct((m, d), out_dtype),
        grid=(m // tr,),
        in_specs=[pl.BlockSpec((tr, d), lambda i: (i, 0)),
                  pl.BlockSpec((1, d), lambda i: (0, 0))],
        out_specs=pl.BlockSpec((tr, d), lambda i: (i, 0)),
        compiler_params=_cparams(("parallel",)),
        name="rmsnorm",
    )(x, w.reshape(1, d))


def _mm_kernel(*refs, n_pairs, has_res, nk, act):
    pairs = refs[:2 * n_pairs]
    pos = 2 * n_pairs
    res_ref = refs[pos] if has_res else None
    o_ref = refs[pos + int(has_res)]

    def product():
        part = None
        for p in range(n_pairs):
            x = pairs[2 * p][...]
            if x.dtype != BF16:
                x = x.astype(BF16)
            d = jnp.dot(x, pairs[2 * p + 1][...], preferred_element_type=F32)
            part = d if part is None else part + d
        return part

    if nk == 1:
        part = product()
        if act == "relu2":
            part = jnp.square(jnp.maximum(part, 0.0))
        if has_res:
            part = res_ref[...] + part
        o_ref[...] = part.astype(o_ref.dtype)
    else:
        k = pl.program_id(2)

        @pl.when(k == 0)
        def _():
            o_ref[...] = (res_ref[...] + product()) if has_res else product()

        @pl.when(k > 0)
        def _():
            o_ref[...] += product()


def _matmul(pairs, *, n_out, w_col0=0, res=None, act=None, out_dtype=F32,
            tm=1024, tn=1024, tk=None, name="matmul"):
    m, kdim = pairs[0][0].shape
    tm = _tile(m, tm, SUBLANES)
    tn = _tile(n_out, tn, LANES, also=(w_col0,))
    tk = kdim if tk is None else _tile(kdim, tk, LANES)
    nk = kdim // tk
    if nk > 1:
        assert act is None and out_dtype == F32
    c0 = w_col0 // tn
    in_specs, args = [], []
    for x, w in pairs:
        assert x.shape == (m, kdim) and w.shape[0] == kdim
        in_specs += [pl.BlockSpec((tm, tk), lambda i, j, k: (i, k)),
                     pl.BlockSpec((tk, tn), lambda i, j, k: (k, j + c0))]
        args += [x, w]
    if res is not None:
        in_specs.append(pl.BlockSpec((tm, tn), lambda i, j, k: (i, j)))
        args.append(res)
    kern = functools.partial(_mm_kernel, n_pairs=len(pairs), has_res=res is not None, nk=nk, act=act)
    return pl.pallas_call(
        kern,
        out_shape=jax.ShapeDtypeStruct((m, n_out), out_dtype),
        grid=(m // tm, n_out // tn, nk),
        in_specs=in_specs,
        out_specs=pl.BlockSpec((tm, tn), lambda i, j, k: (i, j)),
        compiler_params=_cparams(("parallel", "parallel", "arbitrary")),
        name=name,
    )(*args)


def _mmw_kernel(*refs, n_x, has_res, act, tn, c0, nj, w_is_transposed, w_row0, kdim):
    xs = refs[:n_x]
    w_any = refs[n_x]
    xds = refs[n_x + 1:2 * n_x + 1]
    pos = 2 * n_x + 1
    res_ref = resd_ref = None
    if has_res:
        res_ref, resd_ref = refs[pos], refs[pos + 1]
        pos += 2
    o_ref, od_ref, wf_scr, wb_scr, sem = refs[pos:pos + 5]
    j, i = pl.program_id(0), pl.program_id(1)

    def w_copy(jj):
        col = pl.multiple_of((jj + c0) * tn, tn)
        rows = pl.ds(w_row0, kdim)
        panel = w_any.at[pl.ds(col, tn), rows] if w_is_transposed else w_any.at[rows, pl.ds(col, tn)]
        return pltpu.make_async_copy(panel, wf_scr, sem)

    def product(x_refs):
        part, k0 = None, 0
        for xr in x_refs:
            x = xr[...]
            if x.dtype != BF16:
                x = x.astype(BF16)
            kp = x.shape[1]
            d = jnp.dot(x, wb_scr[k0:k0 + kp, :], preferred_element_type=F32)
            part = d if part is None else part + d
            k0 += kp
        return part

    def finish(part, r_ref, out_ref):
        if act == "relu2":
            part = jnp.square(jnp.maximum(part, 0.0))
        if r_ref is not None:
            part = r_ref[...] + part
        out_ref[...] = part.astype(out_ref.dtype)

    @pl.when(i == 0)
    def _():
        @pl.when(j == 0)
        def _():
            w_copy(0).start()

        w_copy(j).wait()
        panel = wf_scr[...]
        wb_scr[...] = (panel.T if w_is_transposed else panel).astype(BF16)

        @pl.when(j + 1 < nj)
        def _():
            w_copy(j + 1).start()

        finish(product(xds), resd_ref, od_ref)

    finish(product(xs), res_ref, o_ref)


def _matmul_w32(xs, xds, w, *, n_out, w_col0=0, res=None, resd=None, act=None, out_dtype=F32,
                tm=512, tn=1024, w_is_transposed=False, k_chunk=None, name="matmul_w32"):
    m, md = xs[0].shape[0], xds[0].shape[0]
    if k_chunk is None:
        kdim = w.shape[1] if w_is_transposed else w.shape[0]
        widths, xcb, w_row0 = [x.shape[1] for x in xs], 0, 0
        assert sum(widths) == kdim and widths == [x.shape[1] for x in xds]
    else:
        xcb, kdim = k_chunk
        widths, w_row0 = [kdim], xcb * kdim
        assert len(xs) == 1 and xs[0].shape[1] % kdim == 0
    tm = _tile(m, tm, SUBLANES)
    tn = _tile(n_out, tn, LANES, also=(w_col0,))
    nj = n_out // tn
    in_specs = [pl.BlockSpec((tm, kp), lambda j, i: (i, xcb)) for kp in widths]
    in_specs.append(pl.BlockSpec(memory_space=pl.ANY))
    in_specs += [pl.BlockSpec((md, kp), lambda j, i: (0, xcb)) for kp in widths]
    args = [*xs, w, *xds]
    if res is not None:
        in_specs += [pl.BlockSpec((tm, tn), lambda j, i: (i, j)), pl.BlockSpec((md, tn), lambda j, i: (0, j))]
        args += [res, resd]
    kern = functools.partial(_mmw_kernel, n_x=len(xs), has_res=res is not None, act=act, tn=tn,
                             c0=w_col0 // tn, nj=nj, w_is_transposed=w_is_transposed, w_row0=w_row0,
                             kdim=kdim)
    wf_shape = (tn, kdim) if w_is_transposed else (kdim, tn)
    return pl.pallas_call(
        kern,
        out_shape=(jax.ShapeDtypeStruct((m, n_out), out_dtype), jax.ShapeDtypeStruct((md, n_out), out_dtype)),
        grid=(nj, m // tm),
        in_specs=in_specs,
        out_specs=(pl.BlockSpec((tm, tn), lambda j, i: (i, j)), pl.BlockSpec((md, tn), lambda j, i: (0, j))),
        scratch_shapes=[pltpu.VMEM(wf_shape, F32), pltpu.VMEM((kdim, tn), BF16), pltpu.SemaphoreType.DMA],
        compiler_params=_cparams(("arbitrary", "arbitrary")),
        name=name,
    )(*args)


def _dt_proj_kernel(h_ref, w_ref, o_ref):
    n_dt = w_ref.shape[0]
    dt = _dot_nt(h_ref[...], w_ref[...].astype(BF16))
    o_ref[...] = jnp.concatenate([dt, jnp.zeros((dt.shape[0], LANES - n_dt), F32)], axis=1)


def _dt_proj(h, w_t, row0, n_dt):
    m, kdim = h.shape
    tm = _tile(m, 1024, SUBLANES)
    assert row0 % n_dt == 0 and n_dt % SUBLANES == 0
    return pl.pallas_call(
        _dt_proj_kernel,
        out_shape=jax.ShapeDtypeStruct((m, LANES), F32),
        grid=(m // tm,),
        in_specs=[pl.BlockSpec((tm, kdim), lambda i: (i, 0)),
                  pl.BlockSpec((n_dt, kdim), lambda i: (row0 // n_dt, 0))],
        out_specs=pl.BlockSpec((tm, LANES), lambda i: (i, 0)),
        compiler_params=_cparams(("parallel",)),
        name="in_proj_dt",
    )(h, w_t)


def _norm_rope(x, w, cos, sa, sb, lo):
    x2 = x * x
    s_lo = jnp.sum(jnp.where(lo, x2, 0.0), axis=-1, keepdims=True)
    s_hi = jnp.sum(jnp.where(lo, 0.0, x2), axis=-1, keepdims=True)
    ms = jnp.where(lo, s_lo, s_hi) * (1.0 / ATTN_SUB)
    y = x * lax.rsqrt(ms + EPS) * w
    up = pltpu.roll(y, LANES - ROT_DIM // 2, 1)
    dn = pltpu.roll(y, ROT_DIM // 2, 1)
    return y * cos + up * sa + dn * sb


def _qkv_post_decode_kernel(qkv_ref, cos_ref, sa_ref, sb_ref, qw_ref, kw_ref, q_ref, kf_ref, vf_ref,
                            *, d_q, d_k):
    tr = qkv_ref.shape[0]
    cos, sa, sb = cos_ref[...], sa_ref[...], sb_ref[...]
    lo = lax.broadcasted_iota(jnp.int32, (tr, LANES), 1) < ATTN_SUB
    qw, kw = qw_ref[...], kw_ref[...]
    scale = ATTN_SUB ** -0.5
    for g in range(d_q // LANES):
        sl = slice(g * LANES, (g + 1) * LANES)
        q_ref[:, sl] = _norm_rope(qkv_ref[:, sl], qw, cos, sa, sb, lo) * scale
    for g in range(d_k // LANES):
        sl = slice(g * LANES, (g + 1) * LANES)
        kf_ref[:, sl] = _norm_rope(qkv_ref[:, d_q + g * LANES:d_q + (g + 1) * LANES], kw, cos, sa, sb, lo)
    vf_ref[...] = qkv_ref[:, d_q + d_k:]


def _qkv_post_prompt_kernel(qkv_ref, cos_ref, sa_ref, sb_ref, qw_ref, kw_ref,
                            qt_ref, kf_ref, k0_ref, k1_ref, vf_ref, vt_ref, *, d_q, d_k):
    tr = qkv_ref.shape[0]
    cos, sa, sb = cos_ref[...], sa_ref[...], sb_ref[...]
    lo = lax.broadcasted_iota(jnp.int32, (tr, LANES), 1) < ATTN_SUB
    qw, kw = qw_ref[...], kw_ref[...]
    scale = (ATTN_SUB ** -0.5) * math.log2(math.e)
    for hd in range(d_q // LANES):
        y = _norm_rope(qkv_ref[:, hd * LANES:(hd + 1) * LANES], qw, cos, sa, sb, lo) * scale
        qt_ref[hd] = y.T.astype(BF16)
    for g in range(d_k // LANES):
        y = _norm_rope(qkv_ref[:, d_q + g * LANES:d_q + (g + 1) * LANES], kw, cos, sa, sb, lo)
        sl = slice(g * LANES, (g + 1) * LANES)
        kf_ref[:, sl] = y
        k0_ref[:, sl] = jnp.where(lo, y, 0.0).astype(BF16)
        k1_ref[:, sl] = jnp.where(lo, 0.0, y).astype(BF16)
    v = qkv_ref[:, d_q + d_k:]
    vf_ref[...] = v
    for g in range(v.shape[1] // LANES):
        vt_ref[g] = v[:, g * LANES:(g + 1) * LANES].T.astype(BF16)


def _rope_tables(pos):
    half = ROT_DIM // 2
    inv = jnp.exp(-math.log(ROPE_THETA) * jnp.arange(half, dtype=F32) * 2.0 / ROT_DIM)
    ang = pos.astype(F32)[:, None] * inv[None, :]
    cos, sin = jnp.cos(ang), jnp.sin(ang)
    n = pos.shape[0]
    ones = jnp.ones((n, ATTN_SUB - ROT_DIM), F32)
    zeros8 = jnp.zeros((n, half), F32)
    zeros = jnp.zeros((n, ATTN_SUB - ROT_DIM), F32)
    c = jnp.concatenate([cos, cos, ones], axis=1)
    sa = jnp.concatenate([-sin, zeros8, zeros], axis=1)
    sb = jnp.concatenate([zeros8, sin, zeros], axis=1)
    rep = LANES // ATTN_SUB
    return jnp.tile(c, (1, rep)), jnp.tile(sa, (1, rep)), jnp.tile(sb, (1, rep))


def _qkv_post(qkv, tables, q_norm_w, k_norm_w, *, d_q, d_k, d_v, seq_blocks=None):
    m = qkv.shape[0]
    n_tab = tables[0].shape[0]
    rep = LANES // ATTN_SUB
    row = lambda i: (i, 0)
    fix = lambda i: (0, 0)
    if seq_blocks is None:
        tr = _tile(m, 256, SUBLANES, also=(n_tab,))
    else:
        tr = n_tab // seq_blocks
    ntb = n_tab // tr
    tab = lambda i: (i % ntb, 0)
    in_specs = [pl.BlockSpec((tr, d_q + d_k + d_v), row),
                pl.BlockSpec((tr, LANES), tab), pl.BlockSpec((tr, LANES), tab),
                pl.BlockSpec((tr, LANES), tab),
                pl.BlockSpec((1, LANES), fix), pl.BlockSpec((1, LANES), fix)]
    args = (qkv, *tables, jnp.tile(q_norm_w, rep).reshape(1, LANES), jnp.tile(k_norm_w, rep).reshape(1, LANES))
    if seq_blocks is None:
        return pl.pallas_call(
            functools.partial(_qkv_post_decode_kernel, d_q=d_q, d_k=d_k),
            out_shape=(jax.ShapeDtypeStruct((m, d_q), F32), jax.ShapeDtypeStruct((m, d_k), F32),
                       jax.ShapeDtypeStruct((m, d_v), F32)),
            grid=(m // tr,),
            in_specs=in_specs,
            out_specs=(pl.BlockSpec((tr, d_q), row), pl.BlockSpec((tr, d_k), row),
                       pl.BlockSpec((tr, d_v), row)),
            compiler_params=_cparams(("parallel",)),
            name="qkv_post_decode",
        )(*args)
    batch = m // n_tab
    n_heads, n_kv = d_q // LANES, d_v // LANES
    tmap = lambda i: (i // seq_blocks, 0, i % seq_blocks, 0, 0)
    return pl.pallas_call(
        functools.partial(_qkv_post_prompt_kernel, d_q=d_q, d_k=d_k),
        out_shape=(jax.ShapeDtypeStruct((batch, n_heads, seq_blocks, LANES, tr), BF16),
                   jax.ShapeDtypeStruct((m, d_k), F32),
                   jax.ShapeDtypeStruct((m, d_k), BF16), jax.ShapeDtypeStruct((m, d_k), BF16),
                   jax.ShapeDtypeStruct((m, d_v), F32),
                   jax.ShapeDtypeStruct((batch, n_kv, seq_blocks, LANES, tr), BF16)),
        grid=(m // tr,),
        in_specs=in_specs,
        out_specs=(pl.BlockSpec((None, n_heads, None, LANES, tr), tmap),
                   pl.BlockSpec((tr, d_k), row), pl.BlockSpec((tr, d_k), row), pl.BlockSpec((tr, d_k), row),
                   pl.BlockSpec((tr, d_v), row),
                   pl.BlockSpec((None, n_kv, None, LANES, tr), tmap)),
        compiler_params=_cparams(("parallel",)),
        name="qkv_post_prompt",
    )(*args)


def _lambda_value(lam_ref, lam_init):
    l = lam_ref[...]
    d1 = jnp.sum(l[0:1] * l[1:2], axis=-1, keepdims=True)
    d2 = jnp.sum(l[2:3] * l[3:4], axis=-1, keepdims=True)
    return jnp.exp(d1) - jnp.exp(d2) + lam_init


def _subln(o, w, lam_init):
    ms = jnp.mean(o * o, axis=-1, keepdims=True)
    return (o * lax.rsqrt(ms + EPS) * w) * (1.0 - lam_init)


def _dot_nt(a, b):
    return lax.dot_general(a, b, (((1,), (1,)), ((), ())), preferred_element_type=F32)


ONES_ROWS = 16
ATTN_BLOCK = 256


def _attn_prompt_kernel(qt_ref, k0_ref, k1_ref, vt_ref, lam_ref, subw_ref, o_ref, acc_scr,
                        *, blk, lam_init):
    i = pl.program_id(2)
    acc_scr[...] = jnp.zeros(acc_scr.shape, F32)
    k_refs = (k0_ref, k1_ref)

    def kv_span(j, n_blk, m_all, diagonal):
        nkeys = n_blk * blk
        v_blocks = jnp.concatenate([vt_ref[j + t] for t in range(n_blk)], axis=1)
        vt = jnp.concatenate([v_blocks, jnp.ones((ONES_ROWS, nkeys), BF16)], axis=0)
        start = pl.multiple_of(j * blk, blk)
        if diagonal:
            visible = (lax.broadcasted_iota(jnp.int32, (nkeys, blk), 0) - (nkeys - blk)
                       <= lax.broadcasted_iota(jnp.int32, (nkeys, blk), 1))
        kcs = [k_refs[c][pl.ds(start, nkeys), :] for c in range(2)]
        hcs = [(h, c) for h in range(KV_GROUP) for c in range(2)]
        scores = [jnp.dot(kcs[c], qt_ref[h], preferred_element_type=F32) for h, c in hcs]
        m_rows, alphas, probs = [], [], []
        for idx, s in enumerate(scores):
            if diagonal:
                s = jnp.where(visible, s, NEG_INF)
            m_old = m_all[idx:idx + 1, :]
            m_new = jnp.maximum(m_old, jnp.max(s, axis=0, keepdims=True))
            probs.append(jnp.exp2(s - m_new).astype(BF16))
            alphas.append(jnp.exp2(m_old - m_new))
            m_rows.append(m_new)
        for idx, p in enumerate(probs):
            pv = jnp.dot(vt, p, preferred_element_type=F32)
            acc_scr[idx] = alphas[idx] * acc_scr[idx] + pv
        return jnp.concatenate(m_rows, axis=0)

    m_init = jnp.full((2 * KV_GROUP, blk), NEG_INF, F32)
    m_all = lax.fori_loop(0, i // 2, lambda jp, m: kv_span(2 * jp, 2, m, False), m_init)

    @pl.when(i % 2 == 1)
    def _():
        kv_span(i - 1, 2, m_all, True)

    @pl.when(i % 2 == 0)
    def _():
        kv_span(i, 1, m_all, True)

    lam = _lambda_value(lam_ref, lam_init)
    w = subw_ref[...]
    for h in range(KV_GROUP):
        a0, a1 = acc_scr[2 * h], acc_scr[2 * h + 1]
        o0 = a0[:LANES] / a0[LANES:LANES + 1]
        o1 = a1[:LANES] / a1[LANES:LANES + 1]
        o = _subln((o0 - lam * o1).T, w, lam_init)
        o_ref[:, h * LANES:(h + 1) * LANES] = o.astype(o_ref.dtype)


def _attn_prompt(qt, k0, k1, vt, lam_rows, subln_w, *, lam_init):
    batch, n_heads, nb, _, blk = qt.shape
    n_kv = vt.shape[1]
    seq = nb * blk
    gw = KV_GROUP * LANES
    fix = lambda b, g, i: (0, 0)
    kern = functools.partial(_attn_prompt_kernel, blk=blk, lam_init=lam_init)
    return pl.pallas_call(
        kern,
        out_shape=jax.ShapeDtypeStruct((batch * seq, n_heads * LANES), BF16),
        grid=(batch, n_kv, nb),
        in_specs=[pl.BlockSpec((None, KV_GROUP, None, LANES, blk), lambda b, g, i: (b, g, i, 0, 0)),
                  pl.BlockSpec((seq, LANES), lambda b, g, i: (b, g)),
                  pl.BlockSpec((seq, LANES), lambda b, g, i: (b, g)),
                  pl.BlockSpec((None, None, nb, LANES, blk), lambda b, g, i: (b, g, 0, 0, 0)),
                  pl.BlockSpec((4, ATTN_SUB), fix),
                  pl.BlockSpec((1, LANES), fix)],
        out_specs=pl.BlockSpec((blk, gw), lambda b, g, i: (b * nb + i, g)),
        scratch_shapes=[pltpu.VMEM((2 * KV_GROUP, LANES + ONES_ROWS, blk), F32)],
        compiler_params=_cparams(("parallel", "parallel", "parallel")),
        name="attn_prompt",
    )(qt, k0, k1, vt, lam_rows, subln_w.reshape(1, LANES))


def _attn_decode_kernel(pt_ref, q_ref, kn_ref, vn_ref, lam_ref, subw_ref, *rest,
                        pps, n_steps, dseq, n_kv, lam_init):
    k_refs = rest[:pps]
    v_refs = rest[pps:2 * pps]
    o_ref = rest[2 * pps]
    qf_scr, qb_scr, kc_scr, vc_scr, m_scr, l_scr, acc_scr = rest[2 * pps + 1:]
    del pt_ref
    j = pl.program_id(1)
    page = k_refs[0].shape[0] // n_kv
    rows = qf_scr.shape[0]
    blk = 2 * dseq

    @pl.when(j == 0)
    def _():
        qf_scr[...] = jnp.zeros(qf_scr.shape, F32)
        lo = lax.broadcasted_iota(jnp.int32, (dseq, LANES), 1) < ATTN_SUB
        for g in range(n_kv):
            for h in range(KV_GROUP):
                hd = g * KV_GROUP + h
                qh = q_ref[:, hd * LANES:(hd + 1) * LANES]
                r0 = hd * blk
                qf_scr[r0:r0 + dseq, g * LANES:(g + 1) * LANES] = jnp.where(lo, qh, 0.0)
                qf_scr[r0 + dseq:r0 + blk, g * LANES:(g + 1) * LANES] = jnp.where(lo, 0.0, qh)
        qb_scr[...] = qf_scr[...].astype(BF16)
        m_scr[...] = jnp.full(m_scr.shape, NEG_INF, F32)
        l_scr[...] = jnp.zeros(l_scr.shape, F32)
        acc_scr[...] = jnp.zeros(acc_scr.shape, F32)

    n_grp = max(1, n_kv // 2)
    gr, gc = rows // n_grp, kc_scr.shape[1] // n_grp

    def online_update(k, v, mask=None):
        s = jnp.concatenate([_dot_nt(qb_scr[t * gr:(t + 1) * gr, t * gc:(t + 1) * gc],
                                     k[:, t * gc:(t + 1) * gc]) for t in range(n_grp)], axis=0)
        if mask is not None:
            s = jnp.where(mask, s, NEG_INF)
        m_old = m_scr[...]
        m_new = jnp.maximum(m_old, jnp.max(s, axis=-1, keepdims=True))
        p = jnp.exp(s - jnp.tile(m_new, (1, s.shape[1] // LANES)))
        alpha = jnp.exp(m_old - m_new)
        l_scr[...] = alpha * l_scr[...] + jnp.sum(p, axis=-1, keepdims=True)
        p16 = p.astype(BF16)
        for t in range(n_grp):
            rs, cs = slice(t * gr, (t + 1) * gr), slice(t * gc, (t + 1) * gc)
            pv = jnp.dot(p16[rs], v[:, cs], preferred_element_type=F32)
            acc_scr[rs, cs] = jnp.tile(alpha[rs], (1, gc // LANES)) * acc_scr[rs, cs] + pv
        m_scr[...] = m_new

    @pl.when(j < n_steps)
    def _():
        for t in range(pps):
            for g in range(n_kv):
                head_rows = pl.ds(g, page, stride=n_kv)
                cols = slice(g * LANES, (g + 1) * LANES)
                kc_scr[t * page:(t + 1) * page, cols] = k_refs[t][head_rows, :].astype(BF16)
                vc_scr[t * page:(t + 1) * page, cols] = v_refs[t][head_rows, :].astype(BF16)
        online_update(kc_scr, vc_scr)

    @pl.when(j == n_steps)
    def _():
        kpad = jnp.concatenate([kn_ref[...], jnp.zeros((page - dseq, kn_ref.shape[1]), F32)], axis=0)
        vpad = jnp.concatenate([vn_ref[...], jnp.zeros((page - dseq, vn_ref.shape[1]), F32)], axis=0)
        t_idx = lax.broadcasted_iota(jnp.int32, (rows, page), 0) % dseq
        u_idx = lax.broadcasted_iota(jnp.int32, (rows, page), 1)
        online_update(kpad.astype(BF16), vpad.astype(BF16), mask=u_idx <= t_idx)

        lam = _lambda_value(lam_ref, lam_init)
        w = subw_ref[...]
        for g in range(n_kv):
            for h in range(KV_GROUP):
                hd = g * KV_GROUP + h
                r0 = hd * blk
                cs = slice(g * LANES, (g + 1) * LANES)
                o0 = acc_scr[r0:r0 + dseq, cs] / l_scr[r0:r0 + dseq, :]
                o1 = acc_scr[r0 + dseq:r0 + blk, cs] / l_scr[r0 + dseq:r0 + blk, :]
                o_ref[:, hd * LANES:(hd + 1) * LANES] = _subln(o0 - lam * o1, w, lam_init)


def _attn_decode(q, k_new, v_new, cache_k, cache_v, page_table, lam_rows, subln_w, *, page, dseq, lam_init):
    m, d_q = q.shape
    n_b, n_pages = page_table.shape
    kvw = k_new.shape[1]
    n_kv = kvw // LANES
    pps = _tile(n_pages, 8)
    n_steps = n_pages // pps
    rows = n_kv * KV_GROUP * 2 * dseq

    def page_map(t):
        def f(b, j, pt):
            return (pt[b, jnp.minimum(j, n_steps - 1) * pps + t], 0)
        return f

    tok = lambda b, j, pt: (b, 0)
    fix = lambda b, j, pt: (0, 0)
    kv_specs = lambda: [pl.BlockSpec((page * n_kv, LANES), page_map(t)) for t in range(pps)]
    kern = functools.partial(_attn_decode_kernel, pps=pps, n_steps=n_steps, dseq=dseq, n_kv=n_kv,
                             lam_init=lam_init)
    return pl.pallas_call(
        kern,
        out_shape=jax.ShapeDtypeStruct((m, d_q), F32),
        grid_spec=pltpu.PrefetchScalarGridSpec(
            num_scalar_prefetch=1,
            grid=(n_b, n_steps + 1),
            in_specs=[pl.BlockSpec((dseq, d_q), tok),
                      pl.BlockSpec((dseq, kvw), tok), pl.BlockSpec((dseq, kvw), tok),
                      pl.BlockSpec((4, ATTN_SUB), fix), pl.BlockSpec((1, LANES), fix)]
                     + kv_specs() + kv_specs(),
            out_specs=pl.BlockSpec((dseq, d_q), tok),
            scratch_shapes=[pltpu.VMEM((rows, kvw), F32), pltpu.VMEM((rows, kvw), BF16),
                            pltpu.VMEM((pps * page, kvw), BF16), pltpu.VMEM((pps * page, kvw), BF16),
                            pltpu.VMEM((rows, LANES), F32), pltpu.VMEM((rows, LANES), F32),
                            pltpu.VMEM((rows, kvw), F32)]),
        compiler_params=_cparams(("parallel", "arbitrary")),
        name="attn_decode",
    )(page_table, q, k_new, v_new, lam_rows, subln_w.reshape(1, LANES),
      *([cache_k] * pps), *([cache_v] * pps))


def _silu(x):
    return x * (1.0 / (1.0 + jnp.exp(-x)))


def _conv_taps(xp, w, bias, n_rows):
    acc = None
    for t in range(CONV_WIDTH):
        shift = CONV_WIDTH - 1 - t
        xs = xp if shift == 0 else pltpu.roll(xp, shift, 0)
        term = xs[SUBLANES:] * w[t:t + 1]
        acc = term if acc is None else acc + term
    return _silu(acc + bias)


def _conv_prompt_kernel(x_ref, halo_ref, w_ref, b_ref, o_ref):
    i = pl.program_id(1)
    halo = jnp.where(i > 0, halo_ref[...], 0.0)
    xp = jnp.concatenate([halo, x_ref[...]], axis=0)
    o_ref[...] = _conv_taps(xp, w_ref[...], b_ref[...], x_ref.shape[0])


def _conv_prompt(src, col0, conv_w, conv_b, *, batch, seq):
    m = src.shape[0]
    c = conv_w.shape[1]
    tr = _tile(seq, 256, SUBLANES)
    tc = _tile(c, 1024, LANES, also=(col0,))
    nr = seq // tr
    hb = tr // SUBLANES
    cb0 = col0 // tc
    return pl.pallas_call(
        _conv_prompt_kernel,
        out_shape=jax.ShapeDtypeStruct((m, c), F32),
        grid=(batch, nr, c // tc),
        in_specs=[pl.BlockSpec((tr, tc), lambda b, i, j: (b * nr + i, j + cb0)),
                  pl.BlockSpec((SUBLANES, tc),
                               lambda b, i, j: (jnp.maximum((b * nr + i) * hb - 1, 0), j + cb0)),
                  pl.BlockSpec((CONV_WIDTH, tc), lambda b, i, j: (0, j)),
                  pl.BlockSpec((1, tc), lambda b, i, j: (0, j))],
        out_specs=pl.BlockSpec((tr, tc), lambda b, i, j: (b * nr + i, j)),
        compiler_params=_cparams(("parallel", "parallel", "parallel")),
        name="conv_prompt",
    )(src, src, conv_w, conv_b.reshape(1, c))


def _cumsum_rows(x):
    n = x.shape[0]
    row = lax.broadcasted_iota(jnp.int32, x.shape, 0)
    s = 1
    while s < n:
        x = x + jnp.where(row >= s, pltpu.roll(x, s, 0), 0.0)
        s *= 2
    return x


def _ssd_chunk(xs, bm, cm, dt_raw, z, state_ref, dtb, alog, dskip, normw, *, n_valid):
    cl, d_ssm = xs.shape
    n_heads = d_ssm // SSM_HEAD_DIM
    hpg = n_heads // N_SSM_GROUPS
    gs = d_ssm // N_SSM_GROUPS
    x_dt = dt_raw + dtb
    dtp = jnp.maximum(x_dt, 0.0) + jnp.log1p(jnp.exp(-jnp.abs(x_dt)))
    if n_valid < cl:
        dtp = jnp.where(lax.broadcasted_iota(jnp.int32, dtp.shape, 0) < n_valid, dtp, 0.0)
    a = -jnp.exp(alog)
    cs = _cumsum_rows(dtp * a)
    cs_t = cs.T
    cs_last = cs[cl - 1:cl, :]
    dec_all = jnp.exp(cs_last)
    tri = lax.broadcasted_iota(jnp.int32, (cl, cl), 0) >= lax.broadcasted_iota(jnp.int32, (cl, cl), 1)
    lo = lax.broadcasted_iota(jnp.int32, (cl, LANES), 1) < SSM_HEAD_DIM
    lo_row = lo[0:1]
    hsel = lax.broadcasted_iota(jnp.int32, (2 * SSM_HEAD_DIM, SSM_STATE), 0) < SSM_HEAD_DIM
    n_pairs = n_heads // 2
    group_of = lambda pr: (2 * pr) // hpg

    b16 = [bm[:, g * SSM_STATE:(g + 1) * SSM_STATE].astype(BF16) for g in range(N_SSM_GROUPS)]
    c16 = [cm[:, g * SSM_STATE:(g + 1) * SSM_STATE].astype(BF16) for g in range(N_SSM_GROUPS)]
    cb = [_dot_nt(c16[g], b16[g]) for g in range(N_SSM_GROUPS)]
    states = [state_ref[pr * LANES:(pr + 1) * LANES, :] for pr in range(n_pairs)]
    y_off_raw = [_dot_nt(c16[group_of(pr)], states[pr].astype(BF16)) for pr in range(n_pairs)]

    x_pairs, xdt16, xdec16, scores, dec_in = [], [], [], [], []
    for pr in range(n_pairs):
        h0 = 2 * pr
        full = [jnp.broadcast_to(cs[:, h:h + 1], (cl, LANES)) for h in (h0, h0 + 1)]
        cs_pair = jnp.where(lo, full[0], full[1])
        dt_pair = jnp.where(lo, dtp[:, h0:h0 + 1], dtp[:, h0 + 1:h0 + 2])
        last_pair = jnp.where(lo_row, cs_last[:, h0:h0 + 1], cs_last[:, h0 + 1:h0 + 2])
        x_pair = xs[:, pr * LANES:(pr + 1) * LANES]
        xdt = x_pair * dt_pair
        x_pairs.append(x_pair)
        xdt16.append(xdt.astype(BF16))
        xdec16.append((xdt * jnp.exp(last_pair - cs_pair)).astype(BF16))
        dec_in.append(jnp.exp(cs_pair))
        for e in range(2):
            seg = full[e] - cs_t[h0 + e:h0 + e + 1, :]
            lmat = jnp.exp(jnp.where(tri, seg, NEG_INF))
            scores.append((cb[group_of(pr)] * lmat).astype(BF16))

    y_parts = []
    for pr in range(n_pairs):
        h0 = 2 * pr
        halves = [jnp.dot(scores[h0 + e], xdt16[pr], preferred_element_type=F32) for e in range(2)]
        y_diag = jnp.where(lo, halves[0], halves[1])
        skip = jnp.where(lo_row, dskip[:, h0:h0 + 1], dskip[:, h0 + 1:h0 + 2])
        y_parts.append(y_diag + y_off_raw[pr] * dec_in[pr] + skip * x_pairs[pr])
        new = lax.dot_general(xdec16[pr], b16[group_of(pr)], (((0,), (0,)), ((), ())),
                              preferred_element_type=F32)
        dec = jnp.where(hsel, dec_all[:, h0:h0 + 1], dec_all[:, h0 + 1:h0 + 2])
        state_ref[pr * LANES:(pr + 1) * LANES, :] = dec * states[pr] + new
    y = jnp.concatenate(y_parts, axis=1) * _silu(z)
    outs = []
    for g in range(N_SSM_GROUPS):
        yg = y[:, g * gs:(g + 1) * gs]
        ms = jnp.mean(yg * yg, axis=-1, keepdims=True)
        outs.append(yg * lax.rsqrt(ms + EPS) * normw[:, g * gs:(g + 1) * gs])
    return jnp.concatenate(outs, axis=1)


def _ssd_prompt_kernel(xs_ref, b_ref, c_ref, dt_ref, *rest, n_z):
    z_refs = rest[:n_z]
    dtb_ref, alog_ref, dskip_ref, normw_ref, y_ref, st_ref = rest[n_z:]

    @pl.when(pl.program_id(1) == 0)
    def _():
        st_ref[...] = jnp.zeros(st_ref.shape, F32)

    z = jnp.concatenate([r[...] for r in z_refs], axis=1)
    y = _ssd_chunk(xs_ref[...], b_ref[...], c_ref[...], dt_ref[...], z, st_ref,
                   dtb_ref[...], alog_ref[...], dskip_ref[...], normw_ref[...], n_valid=xs_ref.shape[0])
    y_ref[...] = y.astype(y_ref.dtype)


def _pad_lanes(v):
    return jnp.pad(v, (0, LANES - v.shape[0])).reshape(1, LANES)


def _ssd_prompt(xbc_act, dt, z_src, z_col0, dt_bias, a_log, d_skip, ssm_norm_w, *, batch, seq, d_ssm):
    m = xbc_act.shape[0]
    cl = min(SSD_CHUNK, seq)
    nc = seq // cl
    gn = N_SSM_GROUPS * SSM_STATE
    n_heads = d_ssm // SSM_HEAD_DIM
    row = lambda b, c: (b * nc + c, 0)
    fix = lambda b, c: (0, 0)
    assert d_ssm % gn == 0
    zw = _tile(d_ssm, d_ssm, LANES, also=(z_col0,))
    n_z = d_ssm // zw

    def z_map(t):
        return lambda b, c: (b * nc + c, z_col0 // zw + t)

    return pl.pallas_call(
        functools.partial(_ssd_prompt_kernel, n_z=n_z),
        out_shape=(jax.ShapeDtypeStruct((m, d_ssm), BF16),
                   jax.ShapeDtypeStruct((batch, n_heads * SSM_HEAD_DIM, SSM_STATE), F32)),
        grid=(batch, nc),
        in_specs=[pl.BlockSpec((cl, d_ssm), row),
                  pl.BlockSpec((cl, gn), lambda b, c: (b * nc + c, d_ssm // gn)),
                  pl.BlockSpec((cl, gn), lambda b, c: (b * nc + c, d_ssm // gn + 1)),
                  pl.BlockSpec((cl, LANES), row)]
                 + [pl.BlockSpec((cl, zw), z_map(t)) for t in range(n_z)]
                 + [pl.BlockSpec((1, LANES), fix), pl.BlockSpec((1, LANES), fix),
                    pl.BlockSpec((1, LANES), fix), pl.BlockSpec((1, d_ssm), fix)],
        out_specs=(pl.BlockSpec((cl, d_ssm), row),
                   pl.BlockSpec((None, n_heads * SSM_HEAD_DIM, SSM_STATE), lambda b, c: (b, 0, 0))),
        compiler_params=_cparams(("parallel", "arbitrary")),
        name="ssd_prompt",
    )(xbc_act, xbc_act, xbc_act, dt, *([z_src] * n_z), _pad_lanes(dt_bias), _pad_lanes(a_log),
      _pad_lanes(d_skip), ssm_norm_w.reshape(1, d_ssm))


def _ssd_decode_kernel(xbc_ref, prev_ref, dt_ref, z_ref, st0_ref, cw_ref, cb_ref, dtb_ref, alog_ref,
                       dskip_ref, normw_ref, y_ref, st_ref, *, dseq, d_ssm, cl):
    gn = N_SSM_GROUPS * SSM_STATE
    prev = prev_ref[...]
    c = prev.shape[1]
    xp = jnp.concatenate([jnp.zeros((SUBLANES - (CONV_WIDTH - 1), c), F32), prev, xbc_ref[...]], axis=0)
    act = _conv_taps(xp, cw_ref[...], cb_ref[...], dseq)
    pad = lambda v: jnp.concatenate([v, jnp.zeros((cl - dseq, v.shape[1]), F32)], axis=0)
    act = pad(act)
    st_ref[...] = st0_ref[...]
    y = _ssd_chunk(act[:, :d_ssm], act[:, d_ssm:d_ssm + gn], act[:, d_ssm + gn:], pad(dt_ref[...]),
                   pad(z_ref[...]), st_ref, dtb_ref[...], alog_ref[...], dskip_ref[...], normw_ref[...],
                   n_valid=dseq)
    y_ref[...] = y[:dseq]


def _ssd_decode(xbc, conv_prev, dt, z_src, z_col0, state0, conv_w, conv_b, dt_bias, a_log, d_skip,
                ssm_norm_w, *, dseq, d_ssm):
    m, c = xbc.shape
    n_b = m // dseq
    n_heads = d_ssm // SSM_HEAD_DIM
    st_rows = n_heads * SSM_HEAD_DIM
    cl = SSD_CHUNK
    tok = lambda b: (b, 0)
    fix = lambda b: (0, 0)
    st = lambda b: (b, 0, 0)
    assert z_col0 % d_ssm == 0
    kern = functools.partial(_ssd_decode_kernel, dseq=dseq, d_ssm=d_ssm, cl=cl)
    return pl.pallas_call(
        kern,
        out_shape=(jax.ShapeDtypeStruct((m, d_ssm), F32),
                   jax.ShapeDtypeStruct((n_b, st_rows, SSM_STATE), F32)),
        grid=(n_b,),
        in_specs=[pl.BlockSpec((dseq, c), tok),
                  pl.BlockSpec((None, CONV_WIDTH - 1, c), st),
                  pl.BlockSpec((dseq, LANES), tok),
                  pl.BlockSpec((dseq, d_ssm), lambda b: (b, z_col0 // d_ssm)),
                  pl.BlockSpec((None, st_rows, SSM_STATE), st),
                  pl.BlockSpec((CONV_WIDTH, c), fix), pl.BlockSpec((1, c), fix),
                  pl.BlockSpec((1, LANES), fix), pl.BlockSpec((1, LANES), fix),
                  pl.BlockSpec((1, LANES), fix), pl.BlockSpec((1, d_ssm), fix)],
        out_specs=(pl.BlockSpec((dseq, d_ssm), tok),
                   pl.BlockSpec((None, st_rows, SSM_STATE), st)),
        compiler_params=_cparams(("parallel",)),
        name="ssd_decode",
    )(xbc, conv_prev, dt, z_src, state0.reshape(n_b, st_rows, SSM_STATE), conv_w, conv_b.reshape(1, c),
      _pad_lanes(dt_bias), _pad_lanes(a_log), _pad_lanes(d_skip), ssm_norm_w.reshape(1, d_ssm))


def _layer(xp, xd, *, bp, seq, bd, dseq, pos_p, pos_d, wts, dec, lam_init):
    d_model = xp.shape[1]
    d_attn = d_model // 2
    d_ssm = d_model - d_attn
    n_heads = d_attn // ATTN_VHEAD
    n_kv = max(1, n_heads // KV_GROUP)
    d_q, d_k, d_v = n_heads * LANES, n_kv * LANES, n_kv * ATTN_VHEAD
    conv_dim = d_ssm + 2 * N_SSM_GROUPS * SSM_STATE
    c_z = d_q + d_k + d_v
    c_x = c_z + d_ssm
    c_dt = c_x + conv_dim
    n_ssm_heads = d_ssm // SSM_HEAD_DIM

    hp, hd = _rmsnorm(xp, wts["ln1_w"]), _rmsnorm(xd, wts["ln1_w"])
    proj_p, proj_d = _matmul_w32([hp], [hd], wts["w_in_t"], n_out=c_dt, w_is_transposed=True,
                                 name="in_proj")
    dt_p = _dt_proj(hp, wts["w_in_t"], c_dt, n_ssm_heads)
    dt_d = _dt_proj(hd, wts["w_in_t"], c_dt, n_ssm_heads)

    tab_p = _rope_tables(pos_p)
    tab_d = tuple(jnp.tile(t, (bd, 1)) for t in _rope_tables(pos_d))
    lam_rows = jnp.stack([wts["lambda_q1"], wts["lambda_k1"], wts["lambda_q2"], wts["lambda_k2"]])
    qk = dict(d_q=d_q, d_k=d_k, d_v=d_v)

    blk = _tile(seq, ATTN_BLOCK, LANES)
    qt, kp_f32, k0, k1, vp_f32, vt = _qkv_post(proj_p, tab_p, wts["q_norm_w"], wts["k_norm_w"],
                                               seq_blocks=seq // blk, **qk)
    o_p = _attn_prompt(qt, k0, k1, vt, lam_rows, wts["subln_w"], lam_init=lam_init)
    act = _conv_prompt(proj_p, c_x, wts["conv_w"], wts["conv_b"], batch=bp, seq=seq)
    y_p, ssm_p = _ssd_prompt(act, dt_p, proj_p, c_z, wts["dt_bias"], wts["a_log"], wts["d_skip"],
                             wts["ssm_norm_w"], batch=bp, seq=seq, d_ssm=d_ssm)
    conv_p = proj_p.reshape(bp, seq, c_dt)[:, seq - (CONV_WIDTH - 1):, c_x:]

    q_d, kd_f32, vd_f32 = _qkv_post(proj_d, tab_d, wts["q_norm_w"], wts["k_norm_w"], **qk)
    o_d = _attn_decode(q_d, kd_f32, vd_f32, dec["cache_k"], dec["cache_v"], dec["page_table"],
                       lam_rows, wts["subln_w"], page=dec["page"], dseq=dseq, lam_init=lam_init)
    xbc_d = proj_d[:, c_x:]
    y_d, ssm_d = _ssd_decode(xbc_d, dec["state_conv"], dt_d, proj_d[:, c_z:c_x], 0, dec["state_ssm"],
                             wts["conv_w"], wts["conv_b"], wts["dt_bias"], wts["a_log"], wts["d_skip"],
                             wts["ssm_norm_w"], dseq=dseq, d_ssm=d_ssm)
    conv_d = jnp.concatenate([dec["state_conv"], xbc_d.reshape(bd, dseq, conv_dim)],
                             axis=1)[:, -(CONV_WIDTH - 1):]

    x1_p, x1_d = _matmul_w32([o_p, y_p], [o_d, y_d], wts["w_out"], n_out=d_model, res=xp, resd=xd,
                             name="out_proj")
    h2_p, h2_d = _rmsnorm(x1_p, wts["ln2_w"]), _rmsnorm(x1_d, wts["ln2_w"])
    up_p, up_d = _matmul_w32([h2_p], [h2_d], wts["w_up"], n_out=wts["w_up"].shape[1], act="relu2",
                             out_dtype=BF16, name="ffn_up")
    d_ff = wts["w_down"].shape[0]
    kc = _tile(d_ff, FFN_DOWN_CHUNK, LANES)
    out_p, out_d = x1_p, x1_d
    for c in range(d_ff // kc):
        out_p, out_d = _matmul_w32([up_p], [up_d], wts["w_down"], n_out=d_model, res=out_p, resd=out_d,
                                   k_chunk=(c, kc), name="ffn_down")

    def heads(a, b, s, w):
        return a.reshape(b, s, n_kv, w)

    state = lambda a, b: a.reshape(b, n_ssm_heads, SSM_HEAD_DIM, SSM_STATE)
    return (out_p, out_d,
            (heads(kp_f32, bp, seq, LANES), heads(vp_f32, bp, seq, ATTN_VHEAD), state(ssm_p, bp), conv_p,
             heads(kd_f32, bd, dseq, LANES), heads(vd_f32, bd, dseq, ATTN_VHEAD), state(ssm_d, bd), conv_d))


def kernel(x_prompt, x_sample, cache_k, cache_v, state_ssm, state_conv, page_table, ln1_w, w_in, q_norm_w, k_norm_w, lambda_q1, lambda_k1, lambda_q2, lambda_k2, subln_w, conv_w, conv_b, dt_bias, a_log, d_skip, ssm_norm_w, w_out, ln2_w, w_up, w_down):
    depth = w_in.shape[0]
    bp, seq, d_model = x_prompt.shape
    bs, dseq, _ = x_sample.shape
    page = cache_k.shape[2]
    past_len = page_table.shape[1] * page
    pos_p = jnp.arange(seq, dtype=jnp.int32)
    pos_s = past_len + jnp.arange(dseq, dtype=jnp.int32)

    yp = x_prompt.reshape(bp * seq, d_model)
    ys = x_sample.reshape(bs * dseq, d_model)
    outs = [[] for _ in range(8)]
    for l in range(depth):
        lam_init = 0.8 - 0.6 * math.exp(-0.3 * l)
        w_in_t = w_in[l].T
        wts = dict(
            ln1_w=ln1_w[l], w_in_t=w_in_t,
            q_norm_w=q_norm_w[l], k_norm_w=k_norm_w[l],
            lambda_q1=lambda_q1[l], lambda_k1=lambda_k1[l], lambda_q2=lambda_q2[l], lambda_k2=lambda_k2[l],
            subln_w=subln_w[l], conv_w=conv_w[l], conv_b=conv_b[l], dt_bias=dt_bias[l], a_log=a_log[l],
            d_skip=d_skip[l], ssm_norm_w=ssm_norm_w[l], w_out=w_out[l],
            ln2_w=ln2_w[l], w_up=w_up[l], w_down=w_down[l])
        dec = dict(cache_k=cache_k[l].reshape(-1, cache_k.shape[-1]), cache_v=cache_v[l].reshape(-1, cache_v.shape[-1]),
                   page=page, page_table=page_table, state_conv=state_conv[l], state_ssm=state_ssm[l])
        yp, ys, layer_outs = _layer(yp, ys, bp=bp, seq=seq, bd=bs, dseq=dseq, pos_p=pos_p, pos_d=pos_s,
                                    wts=wts, dec=dec, lam_init=lam_init)
        for lst, val in zip(outs, layer_outs):
            lst.append(val)
    stacked = [jnp.stack(lst) for lst in outs]
    return (yp.reshape(bp, seq, d_model), ys.reshape(bs, dseq, d_model), *stacked)
```

```python
import functools
import math

import jax
import jax.numpy as jnp
from jax import lax
from jax.experimental import pallas as pl
from jax.experimental.pallas import tpu as pltpu

F32 = jnp.float32
BF16 = jnp.bfloat16

EPS = 1e-6
LANES = 128
SUBLANES = 8
VMEM_LIMIT = 60 * 1024 * 1024

ATTN_VHEAD = 128
ATTN_SUB = 64
ROT_DIM = 16
ROPE_THETA = 500000.0
KV_GROUP = 4
SSM_HEAD_DIM = 64
SSM_STATE = 128
N_SSM_GROUPS = 8
CONV_WIDTH = 4
SSD_CHUNK = 128
FFN_DOWN_CHUNK = 4096
NEG_INF = float("-inf")


def _cparams(sem):
    return pltpu.CompilerParams(dimension_semantics=sem, vmem_limit_bytes=VMEM_LIMIT)


def _tile(n, target, mult=1, also=()):
    t = min(n, target)
    while t > 0:
        if n % t == 0 and t % mult == 0 and all(a % t == 0 for a in also):
            return t
        t -= 1
    raise ValueError((n, target, mult, also))


def _rmsnorm_kernel(x_ref, w_ref, o_ref):
    x = x_ref[...]
    ms = jnp.mean(x * x, axis=-1, keepdims=True)
    o_ref[...] = (x * lax.rsqrt(ms + EPS) * w_ref[...]).astype(o_ref.dtype)


def _rmsnorm(x, w, out_dtype=BF16):
    m, d = x.shape
    tr = _tile(m, 256, SUBLANES)
    return pl.pallas_call(
        _rmsnorm_kernel,
        out_shape=jax.ShapeDtypeStruct((m, d), out_dtype),
        grid=(m // tr,),
        in_specs=[pl.BlockSpec((tr, d), lambda i: (i, 0)),
                  pl.BlockSpec((1, d), lambda i: (0, 0))],
        out_specs=pl.BlockSpec((tr, d), lambda i: (i, 0)),
        compiler_params=_cparams(("parallel",)),
        name="rmsnorm",
    )(x, w.reshape(1, d))


def _mm_kernel(*refs, n_pairs, has_res, nk, act):
    pairs = refs[:2 * n_pairs]
    pos = 2 * n_pairs
    res_ref = refs[pos] if has_res else None
    o_ref = refs[pos + int(has_res)]

    def product():
        part = None
        for p in range(n_pairs):
            x = pairs[2 * p][...]
            if x.dtype != BF16:
                x = x.astype(BF16)
            d = jnp.dot(x, pairs[2 * p + 1][...], preferred_element_type=F32)
            part = d if part is None else part + d
        return part

    if nk == 1:
        part = product()
        if act == "relu2":
            part = jnp.square(jnp.maximum(part, 0.0))
        if has_res:
            part = res_ref[...] + part
        o_ref[...] = part.astype(o_ref.dtype)
    else:
        k = pl.program_id(2)

        @pl.when(k == 0)
        def _():
            o_ref[...] = (res_ref[...] + product()) if has_res else product()

        @pl.when(k > 0)
        def _():
            o_ref[...] += product()


def _matmul(pairs, *, n_out, w_col0=0, res=None, act=None, out_dtype=F32,
            tm=1024, tn=1024, tk=None, name="matmul"):
    m, kdim = pairs[0][0].shape
    tm = _tile(m, tm, SUBLANES)
    tn = _tile(n_out, tn, LANES, also=(w_col0,))
    tk = kdim if tk is None else _tile(kdim, tk, LANES)
    nk = kdim // tk
    if nk > 1:
        assert act is None and out_dtype == F32
    c0 = w_col0 // tn
    in_specs, args = [], []
    for x, w in pairs:
        assert x.shape == (m, kdim) and w.shape[0] == kdim
        in_specs += [pl.BlockSpec((tm, tk), lambda i, j, k: (i, k)),
                     pl.BlockSpec((tk, tn), lambda i, j, k: (k, j + c0))]
        args += [x, w]
    if res is not None:
        in_specs.append(pl.BlockSpec((tm, tn), lambda i, j, k: (i, j)))
        args.append(res)
    kern = functools.partial(_mm_kernel, n_pairs=len(pairs), has_res=res is not None, nk=nk, act=act)
    return pl.pallas_call(
        kern,
        out_shape=jax.ShapeDtypeStruct((m, n_out), out_dtype),
        grid=(m // tm, n_out // tn, nk),
        in_specs=in_specs,
        out_specs=pl.BlockSpec((tm, tn), lambda i, j, k: (i, j)),
        compiler_params=_cparams(("parallel", "parallel", "arbitrary")),
        name=name,
    )(*args)


def _mmw_kernel(*refs, n_x, has_res, act, tn, c0, nj, w_is_transposed, w_row0, kdim):
    xs = refs[:n_x]
    w_any = refs[n_x]
    xds = refs[n_x + 1:2 * n_x + 1]
    pos = 2 * n_x + 1
    res_ref = resd_ref = None
    if has_res:
        res_ref, resd_ref = refs[pos], refs[pos + 1]
        pos += 2
    o_ref, od_ref, wf_scr, wb_scr, sem = refs[pos:pos + 5]
    j, i = pl.program_id(0), pl.program_id(1)

    def w_copy(jj):
        col = pl.multiple_of((jj + c0) * tn, tn)
        rows = pl.ds(w_row0, kdim)
        panel = w_any.at[pl.ds(col, tn), rows] if w_is_transposed else w_any.at[rows, pl.ds(col, tn)]
        return pltpu.make_async_copy(panel, wf_scr, sem)

    def product(x_refs):
        part, k0 = None, 0
        for xr in x_refs:
            x = xr[...]
            if x.dtype != BF16:
                x = x.astype(BF16)
            kp = x.shape[1]
            d = jnp.dot(x, wb_scr[k0:k0 + kp, :], preferred_element_type=F32)
            part = d if part is None else part + d
            k0 += kp
        return part

    def finish(part, r_ref, out_ref):
        if act == "relu2":
            part = jnp.square(jnp.maximum(part, 0.0))
        if r_ref is not None:
            part = r_ref[...] + part
        out_ref[...] = part.astype(out_ref.dtype)

    @pl.when(i == 0)
    def _():
        @pl.when(j == 0)
        def _():
            w_copy(0).start()

        w_copy(j).wait()
        panel = wf_scr[...]
        wb_scr[...] = (panel.T if w_is_transposed else panel).astype(BF16)

        @pl.when(j + 1 < nj)
        def _():
            w_copy(j + 1).start()

        finish(product(xds), resd_ref, od_ref)

    finish(product(xs), res_ref, o_ref)


def _matmul_w32(xs, xds, w, *, n_out, w_col0=0, res=None, resd=None, act=None, out_dtype=F32,
                tm=512, tn=1024, w_is_transposed=False, k_chunk=None, name="matmul_w32"):
    m, md = xs[0].shape[0], xds[0].shape[0]
    if k_chunk is None:
        kdim = w.shape[1] if w_is_transposed else w.shape[0]
        widths, xcb, w_row0 = [x.shape[1] for x in xs], 0, 0
        assert sum(widths) == kdim and widths == [x.shape[1] for x in xds]
    else:
        xcb, kdim = k_chunk
        widths, w_row0 = [kdim], xcb * kdim
        assert len(xs) == 1 and xs[0].shape[1] % kdim == 0
    tm = _tile(m, tm, SUBLANES)
    tn = _tile(n_out, tn, LANES, also=(w_col0,))
    nj = n_out // tn
    in_specs = [pl.BlockSpec((tm, kp), lambda j, i: (i, xcb)) for kp in widths]
    in_specs.append(pl.BlockSpec(memory_space=pl.ANY))
    in_specs += [pl.BlockSpec((md, kp), lambda j, i: (0, xcb)) for kp in widths]
    args = [*xs, w, *xds]
    if res is not None:
        in_specs += [pl.BlockSpec((tm, tn), lambda j, i: (i, j)), pl.BlockSpec((md, tn), lambda j, i: (0, j))]
        args += [res, resd]
    kern = functools.partial(_mmw_kernel, n_x=len(xs), has_res=res is not None, act=act, tn=tn,
                             c0=w_col0 // tn, nj=nj, w_is_transposed=w_is_transposed, w_row0=w_row0,
                             kdim=kdim)
    wf_shape = (tn, kdim) if w_is_transposed else (kdim, tn)
    return pl.pallas_call(
        kern,
        out_shape=(jax.ShapeDtypeStruct((m, n_out), out_dtype), jax.ShapeDtypeStruct((md, n_out), out_dtype)),
        grid=(nj, m // tm),
        in_specs=in_specs,
        out_specs=(pl.BlockSpec((tm, tn), lambda j, i: (i, j)), pl.BlockSpec((md, tn), lambda j, i: (0, j))),
        scratch_shapes=[pltpu.VMEM(wf_shape, F32), pltpu.VMEM((kdim, tn), BF16), pltpu.SemaphoreType.DMA],
        compiler_params=_cparams(("arbitrary", "arbitrary")),
        name=name,
    )(*args)


def _dt_proj_kernel(h_ref, w_ref, o_ref):
    n_dt = w_ref.shape[0]
    dt = _dot_nt(h_ref[...], w_ref[...].astype(BF16))
    o_ref[...] = jnp.concatenate([dt, jnp.zeros((dt.shape[0], LANES - n_dt), F32)], axis=1)


def _dt_proj(h, w_t, row0, n_dt):
    m, kdim = h.shape
    tm = _tile(m, 1024, SUBLANES)
    assert row0 % n_dt == 0 and n_dt % SUBLANES == 0
    return pl.pallas_call(
        _dt_proj_kernel,
        out_shape=jax.ShapeDtypeStruct((m, LANES), F32),
        grid=(m // tm,),
        in_specs=[pl.BlockSpec((tm, kdim), lambda i: (i, 0)),
                  pl.BlockSpec((n_dt, kdim), lambda i: (row0 // n_dt, 0))],
        out_specs=pl.BlockSpec((tm, LANES), lambda i: (i, 0)),
        compiler_params=_cparams(("parallel",)),
        name="in_proj_dt",
    )(h, w_t)


def _norm_rope(x, w, cos, sa, sb, lo):
    x2 = x * x
    s_lo = jnp.sum(jnp.where(lo, x2, 0.0), axis=-1, keepdims=True)
    s_hi = jnp.sum(jnp.where(lo, 0.0, x2), axis=-1, keepdims=True)
    ms = jnp.where(lo, s_lo, s_hi) * (1.0 / ATTN_SUB)
    y = x * lax.rsqrt(ms + EPS) * w
    up = pltpu.roll(y, LANES - ROT_DIM // 2, 1)
    dn = pltpu.roll(y, ROT_DIM // 2, 1)
    return y * cos + up * sa + dn * sb


def _qkv_post_decode_kernel(qkv_ref, cos_ref, sa_ref, sb_ref, qw_ref, kw_ref, q_ref, kf_ref, vf_ref,
                            *, d_q, d_k):
    tr = qkv_ref.shape[0]
    cos, sa, sb = cos_ref[...], sa_ref[...], sb_ref[...]
    lo = lax.broadcasted_iota(jnp.int32, (tr, LANES), 1) < ATTN_SUB
    qw, kw = qw_ref[...], kw_ref[...]
    scale = ATTN_SUB ** -0.5
    for g in range(d_q // LANES):
        sl = slice(g * LANES, (g + 1) * LANES)
        q_ref[:, sl] = _norm_rope(qkv_ref[:, sl], qw, cos, sa, sb, lo) * scale
    for g in range(d_k // LANES):
        sl = slice(g * LANES, (g + 1) * LANES)
        kf_ref[:, sl] = _norm_rope(qkv_ref[:, d_q + g * LANES:d_q + (g + 1) * LANES], kw, cos, sa, sb, lo)
    vf_ref[...] = qkv_ref[:, d_q + d_k:]


def _qkv_post_prompt_kernel(qkv_ref, cos_ref, sa_ref, sb_ref, qw_ref, kw_ref,
                            qt_ref, kf_ref, k0_ref, k1_ref, vf_ref, vt_ref, *, d_q, d_k):
    tr = qkv_ref.shape[0]
    cos, sa, sb = cos_ref[...], sa_ref[...], sb_ref[...]
    lo = lax.broadcasted_iota(jnp.int32, (tr, LANES), 1) < ATTN_SUB
    qw, kw = qw_ref[...], kw_ref[...]
    scale = (ATTN_SUB ** -0.5) * math.log2(math.e)
    for hd in range(d_q // LANES):
        y = _norm_rope(qkv_ref[:, hd * LANES:(hd + 1) * LANES], qw, cos, sa, sb, lo) * scale
        qt_ref[hd] = y.T.astype(BF16)
    for g in range(d_k // LANES):
        y = _norm_rope(qkv_ref[:, d_q + g * LANES:d_q + (g + 1) * LANES], kw, cos, sa, sb, lo)
        sl = slice(g * LANES, (g + 1) * LANES)
        kf_ref[:, sl] = y
        k0_ref[:, sl] = jnp.where(lo, y, 0.0).astype(BF16)
        k1_ref[:, sl] = jnp.where(lo, 0.0, y).astype(BF16)
    v = qkv_ref[:, d_q + d_k:]
    vf_ref[...] = v
    for g in range(v.shape[1] // LANES):
        vt_ref[g] = v[:, g * LANES:(g + 1) * LANES].T.astype(BF16)


def _rope_tables(pos):
    half = ROT_DIM // 2
    inv = jnp.exp(-math.log(ROPE_THETA) * jnp.arange(half, dtype=F32) * 2.0 / ROT_DIM)
    ang = pos.astype(F32)[:, None] * inv[None, :]
    cos, sin = jnp.cos(ang), jnp.sin(ang)
    n = pos.shape[0]
    ones = jnp.ones((n, ATTN_SUB - ROT_DIM), F32)
    zeros8 = jnp.zeros((n, half), F32)
    zeros = jnp.zeros((n, ATTN_SUB - ROT_DIM), F32)
    c = jnp.concatenate([cos, cos, ones], axis=1)
    sa = jnp.concatenate([-sin, zeros8, zeros], axis=1)
    sb = jnp.concatenate([zeros8, sin, zeros], axis=1)
    rep = LANES // ATTN_SUB
    return jnp.tile(c, (1, rep)), jnp.tile(sa, (1, rep)), jnp.tile(sb, (1, rep))


def _qkv_post(qkv, tables, q_norm_w, k_norm_w, *, d_q, d_k, d_v, seq_blocks=None):
    m = qkv.shape[0]
    n_tab = tables[0].shape[0]
    rep = LANES // ATTN_SUB
    row = lambda i: (i, 0)
    fix = lambda i: (0, 0)
    if seq_blocks is None:
        tr = _tile(m, 256, SUBLANES, also=(n_tab,))
    else:
        tr = n_tab // seq_blocks
    ntb = n_tab // tr
    tab = lambda i: (i % ntb, 0)
    in_specs = [pl.BlockSpec((tr, d_q + d_k + d_v), row),
                pl.BlockSpec((tr, LANES), tab), pl.BlockSpec((tr, LANES), tab),
                pl.BlockSpec((tr, LANES), tab),
                pl.BlockSpec((1, LANES), fix), pl.BlockSpec((1, LANES), fix)]
    args = (qkv, *tables, jnp.tile(q_norm_w, rep).reshape(1, LANES), jnp.tile(k_norm_w, rep).reshape(1, LANES))
    if seq_blocks is None:
        return pl.pallas_call(
            functools.partial(_qkv_post_decode_kernel, d_q=d_q, d_k=d_k),
            out_shape=(jax.ShapeDtypeStruct((m, d_q), F32), jax.ShapeDtypeStruct((m, d_k), F32),
                       jax.ShapeDtypeStruct((m, d_v), F32)),
            grid=(m // tr,),
            in_specs=in_specs,
            out_specs=(pl.BlockSpec((tr, d_q), row), pl.BlockSpec((tr, d_k), row),
                       pl.BlockSpec((tr, d_v), row)),
            compiler_params=_cparams(("parallel",)),
            name="qkv_post_decode",
        )(*args)
    batch = m // n_tab
    n_heads, n_kv = d_q // LANES, d_v // LANES
    tmap = lambda i: (i // seq_blocks, 0, i % seq_blocks, 0, 0)
    return pl.pallas_call(
        functools.partial(_qkv_post_prompt_kernel, d_q=d_q, d_k=d_k),
        out_shape=(jax.ShapeDtypeStruct((batch, n_heads, seq_blocks, LANES, tr), BF16),
                   jax.ShapeDtypeStruct((m, d_k), F32),
                   jax.ShapeDtypeStruct((m, d_k), BF16), jax.ShapeDtypeStruct((m, d_k), BF16),
                   jax.ShapeDtypeStruct((m, d_v), F32),
                   jax.ShapeDtypeStruct((batch, n_kv, seq_blocks, LANES, tr), BF16)),
        grid=(m // tr,),
        in_specs=in_specs,
        out_specs=(pl.BlockSpec((None, n_heads, None, LANES, tr), tmap),
                   pl.BlockSpec((tr, d_k), row), pl.BlockSpec((tr, d_k), row), pl.BlockSpec((tr, d_k), row),
                   pl.BlockSpec((tr, d_v), row),
                   pl.BlockSpec((None, n_kv, None, LANES, tr), tmap)),
        compiler_params=_cparams(("parallel",)),
        name="qkv_post_prompt",
    )(*args)


def _lambda_value(lam_ref, lam_init):
    l = lam_ref[...]
    d1 = jnp.sum(l[0:1] * l[1:2], axis=-1, keepdims=True)
    d2 = jnp.sum(l[2:3] * l[3:4], axis=-1, keepdims=True)
    return jnp.exp(d1) - jnp.exp(d2) + lam_init


def _subln(o, w, lam_init):
    ms = jnp.mean(o * o, axis=-1, keepdims=True)
    return (o * lax.rsqrt(ms + EPS) * w) * (1.0 - lam_init)


def _dot_nt(a, b):
    return lax.dot_general(a, b, (((1,), (1,)), ((), ())), preferred_element_type=F32)


ONES_ROWS = 16
ATTN_BLOCK = 256


def _attn_prompt_kernel(qt_ref, k0_ref, k1_ref, vt_ref, lam_ref, subw_ref, o_ref, acc_scr,
                        *, blk, lam_init):
    i = pl.program_id(2)
    acc_scr[...] = jnp.zeros(acc_scr.shape, F32)
    k_refs = (k0_ref, k1_ref)

    def kv_span(j, n_blk, m_all, diagonal):
        nkeys = n_blk * blk
        v_blocks = jnp.concatenate([vt_ref[j + t] for t in range(n_blk)], axis=1)
        vt = jnp.concatenate([v_blocks, jnp.ones((ONES_ROWS, nkeys), BF16)], axis=0)
        start = pl.multiple_of(j * blk, blk)
        if diagonal:
            visible = (lax.broadcasted_iota(jnp.int32, (nkeys, blk), 0) - (nkeys - blk)
                       <= lax.broadcasted_iota(jnp.int32, (nkeys, blk), 1))
        kcs = [k_refs[c][pl.ds(start, nkeys), :] for c in range(2)]
        hcs = [(h, c) for h in range(KV_GROUP) for c in range(2)]
        scores = [jnp.dot(kcs[c], qt_ref[h], preferred_element_type=F32) for h, c in hcs]
        m_rows, alphas, probs = [], [], []
        for idx, s in enumerate(scores):
            if diagonal:
                s = jnp.where(visible, s, NEG_INF)
            m_old = m_all[idx:idx + 1, :]
            m_new = jnp.maximum(m_old, jnp.max(s, axis=0, keepdims=True))
            probs.append(jnp.exp2(s - m_new).astype(BF16))
            alphas.append(jnp.exp2(m_old - m_new))
            m_rows.append(m_new)
        for idx, p in enumerate(probs):
            pv = jnp.dot(vt, p, preferred_element_type=F32)
            acc_scr[idx] = alphas[idx] * acc_scr[idx] + pv
        return jnp.concatenate(m_rows, axis=0)

    m_init = jnp.full((2 * KV_GROUP, blk), NEG_INF, F32)
    m_all = lax.fori_loop(0, i // 2, lambda jp, m: kv_span(2 * jp, 2, m, False), m_init)

    @pl.when(i % 2 == 1)
    def _():
        kv_span(i - 1, 2, m_all, True)

    @pl.when(i % 2 == 0)
    def _():
        kv_span(i, 1, m_all, True)

    lam = _lambda_value(lam_ref, lam_init)
    w = subw_ref[...]
    for h in range(KV_GROUP):
        a0, a1 = acc_scr[2 * h], acc_scr[2 * h + 1]
        o0 = a0[:LANES] / a0[LANES:LANES + 1]
        o1 = a1[:LANES] / a1[LANES:LANES + 1]
        o = _subln((o0 - lam * o1).T, w, lam_init)
        o_ref[:, h * LANES:(h + 1) * LANES] = o.astype(o_ref.dtype)


def _attn_prompt(qt, k0, k1, vt, lam_rows, subln_w, *, lam_init):
    batch, n_heads, nb, _, blk = qt.shape
    n_kv = vt.shape[1]
    seq = nb * blk
    gw = KV_GROUP * LANES
    fix = lambda b, g, i: (0, 0)
    kern = functools.partial(_attn_prompt_kernel, blk=blk, lam_init=lam_init)
    return pl.pallas_call(
        kern,
        out_shape=jax.ShapeDtypeStruct((batch * seq, n_heads * LANES), BF16),
        grid=(batch, n_kv, nb),
        in_specs=[pl.BlockSpec((None, KV_GROUP, None, LANES, blk), lambda b, g, i: (b, g, i, 0, 0)),
                  pl.BlockSpec((seq, LANES), lambda b, g, i: (b, g)),
                  pl.BlockSpec((seq, LANES), lambda b, g, i: (b, g)),
                  pl.BlockSpec((None, None, nb, LANES, blk), lambda b, g, i: (b, g, 0, 0, 0)),
                  pl.BlockSpec((4, ATTN_SUB), fix),
                  pl.BlockSpec((1, LANES), fix)],
        out_specs=pl.BlockSpec((blk, gw), lambda b, g, i: (b * nb + i, g)),
        scratch_shapes=[pltpu.VMEM((2 * KV_GROUP, LANES + ONES_ROWS, blk), F32)],
        compiler_params=_cparams(("parallel", "parallel", "parallel")),
        name="attn_prompt",
    )(qt, k0, k1, vt, lam_rows, subln_w.reshape(1, LANES))


DECODE_SLOTS = 3


def _attn_decode_kernel(pt_ref, q_ref, kn_ref, vn_ref, lam_ref, subw_ref, ck_any, cv_any, o_ref,
                        qf_scr, qb_scr, kbuf, vbuf, kc_scr, vc_scr, m_scr, l_scr, acc_scr, sem,
                        *, pps, n_steps, n_b, page, dseq, n_kv, lam_init):
    b, j = pl.program_id(0), pl.program_id(1)
    rows = qf_scr.shape[0]
    blk = 2 * dseq
    prow = page * n_kv
    n_chunks = n_b * n_steps

    def chunk_copies(c):
        cb, cj, slot = c // n_steps, c % n_steps, c % DECODE_SLOTS
        copies = []
        for t in range(pps):
            src = pl.ds(pl.multiple_of(pt_ref[cb, cj * pps + t] * prow, prow), prow)
            dst = pl.ds(t * prow, prow)
            copies.append(pltpu.make_async_copy(ck_any.at[src, :], kbuf.at[slot, dst, :], sem.at[0, slot]))
            copies.append(pltpu.make_async_copy(cv_any.at[src, :], vbuf.at[slot, dst, :], sem.at[1, slot]))
        return copies

    @pl.when(j == 0)
    def _():
        qf_scr[...] = jnp.zeros(qf_scr.shape, F32)
        lo = lax.broadcasted_iota(jnp.int32, (dseq, LANES), 1) < ATTN_SUB
        for g in range(n_kv):
            for h in range(KV_GROUP):
                hd = g * KV_GROUP + h
                qh = q_ref[:, hd * LANES:(hd + 1) * LANES]
                r0 = hd * blk
                qf_scr[r0:r0 + dseq, g * LANES:(g + 1) * LANES] = jnp.where(lo, qh, 0.0)
                qf_scr[r0 + dseq:r0 + blk, g * LANES:(g + 1) * LANES] = jnp.where(lo, 0.0, qh)
        qb_scr[...] = qf_scr[...].astype(BF16)
        m_scr[...] = jnp.full(m_scr.shape, NEG_INF, F32)
        l_scr[...] = jnp.zeros(l_scr.shape, F32)
        acc_scr[...] = jnp.zeros(acc_scr.shape, F32)

    n_grp = max(1, n_kv // 2)
    gr, gc = rows // n_grp, kc_scr.shape[1] // n_grp

    def online_update(k, v, mask=None):
        s = jnp.concatenate([_dot_nt(qb_scr[t * gr:(t + 1) * gr, t * gc:(t + 1) * gc],
                                     k[:, t * gc:(t + 1) * gc]) for t in range(n_grp)], axis=0)
        if mask is not None:
            s = jnp.where(mask, s, NEG_INF)
        m_old = m_scr[...]
        m_new = jnp.maximum(m_old, jnp.max(s, axis=-1, keepdims=True))
        p = jnp.exp(s - jnp.tile(m_new, (1, s.shape[1] // LANES)))
        alpha = jnp.exp(m_old - m_new)
        l_scr[...] = alpha * l_scr[...] + jnp.sum(p, axis=-1, keepdims=True)
        p16 = p.astype(BF16)
        for t in range(n_grp):
            rs, cs = slice(t * gr, (t + 1) * gr), slice(t * gc, (t + 1) * gc)
            pv = jnp.dot(p16[rs], v[:, cs], preferred_element_type=F32)
            acc_scr[rs, cs] = jnp.tile(alpha[rs], (1, gc // LANES)) * acc_scr[rs, cs] + pv
        m_scr[...] = m_new

    @pl.when(j < n_steps)
    def _():
        c = b * n_steps + j

        @pl.when(c == 0)
        def _():
            for ahead in range(min(DECODE_SLOTS - 1, n_chunks)):
                for cp in chunk_copies(ahead):
                    cp.start()

        @pl.when(c + DECODE_SLOTS - 1 < n_chunks)
        def _():
            for cp in chunk_copies(c + DECODE_SLOTS - 1):
                cp.start()

        for cp in chunk_copies(c):
            cp.wait()
        slot = c % DECODE_SLOTS
        for t in range(pps):
            for g in range(n_kv):
                head_rows = pl.ds(t * prow + g, page, stride=n_kv)
                cols = slice(g * LANES, (g + 1) * LANES)
                kc_scr[t * page:(t + 1) * page, cols] = kbuf[slot, head_rows, :].astype(BF16)
                vc_scr[t * page:(t + 1) * page, cols] = vbuf[slot, head_rows, :].astype(BF16)
        online_update(kc_scr, vc_scr)

    @pl.when(j == n_steps)
    def _():
        kpad = jnp.concatenate([kn_ref[...], jnp.zeros((page - dseq, kn_ref.shape[1]), F32)], axis=0)
        vpad = jnp.concatenate([vn_ref[...], jnp.zeros((page - dseq, vn_ref.shape[1]), F32)], axis=0)
        t_idx = lax.broadcasted_iota(jnp.int32, (rows, page), 0) % dseq
        u_idx = lax.broadcasted_iota(jnp.int32, (rows, page), 1)
        online_update(kpad.astype(BF16), vpad.astype(BF16), mask=u_idx <= t_idx)

        lam = _lambda_value(lam_ref, lam_init)
        w = subw_ref[...]
        for g in range(n_kv):
            for h in range(KV_GROUP):
                hd = g * KV_GROUP + h
                r0 = hd * blk
                cs = slice(g * LANES, (g + 1) * LANES)
                o0 = acc_scr[r0:r0 + dseq, cs] / l_scr[r0:r0 + dseq, :]
                o1 = acc_scr[r0 + dseq:r0 + blk, cs] / l_scr[r0 + dseq:r0 + blk, :]
                o_ref[:, hd * LANES:(hd + 1) * LANES] = _subln(o0 - lam * o1, w, lam_init)


def _attn_decode(q, k_new, v_new, cache_k, cache_v, page_table, lam_rows, subln_w, *, page, dseq, lam_init):
    m, d_q = q.shape
    n_b, n_pages = page_table.shape
    kvw = k_new.shape[1]
    n_kv = kvw // LANES
    pps = _tile(n_pages, 8)
    n_steps = n_pages // pps
    rows = n_kv * KV_GROUP * 2 * dseq

    tok = lambda b, j, pt: (b, 0)
    fix = lambda b, j, pt: (0, 0)
    chunk_rows = pps * page * n_kv
    kern = functools.partial(_attn_decode_kernel, pps=pps, n_steps=n_steps, n_b=n_b, page=page, dseq=dseq,
                             n_kv=n_kv, lam_init=lam_init)
    return pl.pallas_call(
        kern,
        out_shape=jax.ShapeDtypeStruct((m, d_q), F32),
        grid_spec=pltpu.PrefetchScalarGridSpec(
            num_scalar_prefetch=1,
            grid=(n_b, n_steps + 1),
            in_specs=[pl.BlockSpec((dseq, d_q), tok),
                      pl.BlockSpec((dseq, kvw), tok), pl.BlockSpec((dseq, kvw), tok),
                      pl.BlockSpec((4, ATTN_SUB), fix), pl.BlockSpec((1, LANES), fix),
                      pl.BlockSpec(memory_space=pl.ANY), pl.BlockSpec(memory_space=pl.ANY)],
            out_specs=pl.BlockSpec((dseq, d_q), tok),
            scratch_shapes=[pltpu.VMEM((rows, kvw), F32), pltpu.VMEM((rows, kvw), BF16),
                            pltpu.VMEM((DECODE_SLOTS, chunk_rows, LANES), F32),
                            pltpu.VMEM((DECODE_SLOTS, chunk_rows, LANES), F32),
                            pltpu.VMEM((pps * page, kvw), BF16), pltpu.VMEM((pps * page, kvw), BF16),
                            pltpu.VMEM((rows, LANES), F32), pltpu.VMEM((rows, LANES), F32),
                            pltpu.VMEM((rows, kvw), F32),
                            pltpu.SemaphoreType.DMA((2, DECODE_SLOTS))]),
        compiler_params=_cparams(("arbitrary", "arbitrary")),
        name="attn_decode",
    )(page_table, q, k_new, v_new, lam_rows, subln_w.reshape(1, LANES), cache_k, cache_v)


def _silu(x):
    return x * (1.0 / (1.0 + jnp.exp(-x)))


def _conv_taps(xp, w, bias, n_rows):
    acc = None
    for t in range(CONV_WIDTH):
        shift = CONV_WIDTH - 1 - t
        xs = xp if shift == 0 else pltpu.roll(xp, shift, 0)
        term = xs[SUBLANES:] * w[t:t + 1]
        acc = term if acc is None else acc + term
    return _silu(acc + bias)


def _conv_prompt_kernel(x_ref, halo_ref, w_ref, b_ref, o_ref):
    i = pl.program_id(1)
    halo = jnp.where(i > 0, halo_ref[...], 0.0)
    xp = jnp.concatenate([halo, x_ref[...]], axis=0)
    o_ref[...] = _conv_taps(xp, w_ref[...], b_ref[...], x_ref.shape[0])


def _conv_prompt(src, col0, conv_w, conv_b, *, batch, seq):
    m = src.shape[0]
    c = conv_w.shape[1]
    tr = _tile(seq, 256, SUBLANES)
    tc = _tile(c, 1024, LANES, also=(col0,))
    nr = seq // tr
    hb = tr // SUBLANES
    cb0 = col0 // tc
    return pl.pallas_call(
        _conv_prompt_kernel,
        out_shape=jax.ShapeDtypeStruct((m, c), F32),
        grid=(batch, nr, c // tc),
        in_specs=[pl.BlockSpec((tr, tc), lambda b, i, j: (b * nr + i, j + cb0)),
                  pl.BlockSpec((SUBLANES, tc),
                               lambda b, i, j: (jnp.maximum((b * nr + i) * hb - 1, 0), j + cb0)),
                  pl.BlockSpec((CONV_WIDTH, tc), lambda b, i, j: (0, j)),
                  pl.BlockSpec((1, tc), lambda b, i, j: (0, j))],
        out_specs=pl.BlockSpec((tr, tc), lambda b, i, j: (b * nr + i, j)),
        compiler_params=_cparams(("parallel", "parallel", "parallel")),
        name="conv_prompt",
    )(src, src, conv_w, conv_b.reshape(1, c))


def _cumsum_rows(x):
    n = x.shape[0]
    row = lax.broadcasted_iota(jnp.int32, x.shape, 0)
    s = 1
    while s < n:
        x = x + jnp.where(row >= s, pltpu.roll(x, s, 0), 0.0)
        s *= 2
    return x


def _ssd_chunk(xs, bm, cm, dt_raw, z, state_ref, dtb, alog, dskip, normw, *, n_valid):
    cl, d_ssm = xs.shape
    n_heads = d_ssm // SSM_HEAD_DIM
    hpg = n_heads // N_SSM_GROUPS
    gs = d_ssm // N_SSM_GROUPS
    x_dt = dt_raw + dtb
    dtp = jnp.maximum(x_dt, 0.0) + jnp.log1p(jnp.exp(-jnp.abs(x_dt)))
    if n_valid < cl:
        dtp = jnp.where(lax.broadcasted_iota(jnp.int32, dtp.shape, 0) < n_valid, dtp, 0.0)
    a = -jnp.exp(alog)
    cs = _cumsum_rows(dtp * a)
    cs_t = cs.T
    cs_last = cs[cl - 1:cl, :]
    dec_all = jnp.exp(cs_last)
    tri = lax.broadcasted_iota(jnp.int32, (cl, cl), 0) >= lax.broadcasted_iota(jnp.int32, (cl, cl), 1)
    lo = lax.broadcasted_iota(jnp.int32, (cl, LANES), 1) < SSM_HEAD_DIM
    lo_row = lo[0:1]
    hsel = lax.broadcasted_iota(jnp.int32, (2 * SSM_HEAD_DIM, SSM_STATE), 0) < SSM_HEAD_DIM
    n_pairs = n_heads // 2
    group_of = lambda pr: (2 * pr) // hpg

    b16 = [bm[:, g * SSM_STATE:(g + 1) * SSM_STATE].astype(BF16) for g in range(N_SSM_GROUPS)]
    c16 = [cm[:, g * SSM_STATE:(g + 1) * SSM_STATE].astype(BF16) for g in range(N_SSM_GROUPS)]
    cb = [_dot_nt(c16[g], b16[g]) for g in range(N_SSM_GROUPS)]
    states = [state_ref[pr * LANES:(pr + 1) * LANES, :] for pr in range(n_pairs)]
    y_off_raw = [_dot_nt(c16[group_of(pr)], states[pr].astype(BF16)) for pr in range(n_pairs)]

    x_pairs, xdt16, xdec16, scores, dec_in = [], [], [], [], []
    for pr in range(n_pairs):
        h0 = 2 * pr
        full = [jnp.broadcast_to(cs[:, h:h + 1], (cl, LANES)) for h in (h0, h0 + 1)]
        cs_pair = jnp.where(lo, full[0], full[1])
        dt_pair = jnp.where(lo, dtp[:, h0:h0 + 1], dtp[:, h0 + 1:h0 + 2])
        last_pair = jnp.where(lo_row, cs_last[:, h0:h0 + 1], cs_last[:, h0 + 1:h0 + 2])
        x_pair = xs[:, pr * LANES:(pr + 1) * LANES]
        xdt = x_pair * dt_pair
        x_pairs.append(x_pair)
        xdt16.append(xdt.astype(BF16))
        xdec16.append((xdt * jnp.exp(last_pair - cs_pair)).astype(BF16))
        dec_in.append(jnp.exp(cs_pair))
        for e in range(2):
            seg = full[e] - cs_t[h0 + e:h0 + e + 1, :]
            lmat = jnp.exp(jnp.where(tri, seg, NEG_INF))
            scores.append((cb[group_of(pr)] * lmat).astype(BF16))

    y_parts = []
    for pr in range(n_pairs):
        h0 = 2 * pr
        halves = [jnp.dot(scores[h0 + e], xdt16[pr], preferred_element_type=F32) for e in range(2)]
        y_diag = jnp.where(lo, halves[0], halves[1])
        skip = jnp.where(lo_row, dskip[:, h0:h0 + 1], dskip[:, h0 + 1:h0 + 2])
        y_parts.append(y_diag + y_off_raw[pr] * dec_in[pr] + skip * x_pairs[pr])
        new = lax.dot_general(xdec16[pr], b16[group_of(pr)], (((0,), (0,)), ((), ())),
                              preferred_element_type=F32)
        dec = jnp.where(hsel, dec_all[:, h0:h0 + 1], dec_all[:, h0 + 1:h0 + 2])
        state_ref[pr * LANES:(pr + 1) * LANES, :] = dec * states[pr] + new
    y = jnp.concatenate(y_parts, axis=1) * _silu(z)
    outs = []
    for g in range(N_SSM_GROUPS):
        yg = y[:, g * gs:(g + 1) * gs]
        ms = jnp.mean(yg * yg, axis=-1, keepdims=True)
        outs.append(yg * lax.rsqrt(ms + EPS) * normw[:, g * gs:(g + 1) * gs])
    return jnp.concatenate(outs, axis=1)


def _ssd_prompt_kernel(xs_ref, b_ref, c_ref, dt_ref, *rest, n_z):
    z_refs = rest[:n_z]
    dtb_ref, alog_ref, dskip_ref, normw_ref, y_ref, st_ref = rest[n_z:]

    @pl.when(pl.program_id(1) == 0)
    def _():
        st_ref[...] = jnp.zeros(st_ref.shape, F32)

    z = jnp.concatenate([r[...] for r in z_refs], axis=1)
    y = _ssd_chunk(xs_ref[...], b_ref[...], c_ref[...], dt_ref[...], z, st_ref,
                   dtb_ref[...], alog_ref[...], dskip_ref[...], normw_ref[...], n_valid=xs_ref.shape[0])
    y_ref[...] = y.astype(y_ref.dtype)


def _pad_lanes(v):
    return jnp.pad(v, (0, LANES - v.shape[0])).reshape(1, LANES)


def _ssd_prompt(xbc_act, dt, z_src, z_col0, dt_bias, a_log, d_skip, ssm_norm_w, *, batch, seq, d_ssm):
    m = xbc_act.shape[0]
    cl = min(SSD_CHUNK, seq)
    nc = seq // cl
    gn = N_SSM_GROUPS * SSM_STATE
    n_heads = d_ssm // SSM_HEAD_DIM
    row = lambda b, c: (b * nc + c, 0)
    fix = lambda b, c: (0, 0)
    assert d_ssm % gn == 0
    zw = _tile(d_ssm, d_ssm, LANES, also=(z_col0,))
    n_z = d_ssm // zw

    def z_map(t):
        return lambda b, c: (b * nc + c, z_col0 // zw + t)

    return pl.pallas_call(
        functools.partial(_ssd_prompt_kernel, n_z=n_z),
        out_shape=(jax.ShapeDtypeStruct((m, d_ssm), BF16),
                   jax.ShapeDtypeStruct((batch, n_heads * SSM_HEAD_DIM, SSM_STATE), F32)),
        grid=(batch, nc),
        in_specs=[pl.BlockSpec((cl, d_ssm), row),
                  pl.BlockSpec((cl, gn), lambda b, c: (b * nc + c, d_ssm // gn)),
                  pl.BlockSpec((cl, gn), lambda b, c: (b * nc + c, d_ssm // gn + 1)),
                  pl.BlockSpec((cl, LANES), row)]
                 + [pl.BlockSpec((cl, zw), z_map(t)) for t in range(n_z)]
                 + [pl.BlockSpec((1, LANES), fix), pl.BlockSpec((1, LANES), fix),
                    pl.BlockSpec((1, LANES), fix), pl.BlockSpec((1, d_ssm), fix)],
        out_specs=(pl.BlockSpec((cl, d_ssm), row),
                   pl.BlockSpec((None, n_heads * SSM_HEAD_DIM, SSM_STATE), lambda b, c: (b, 0, 0))),
        compiler_params=_cparams(("parallel", "arbitrary")),
        name="ssd_prompt",
    )(xbc_act, xbc_act, xbc_act, dt, *([z_src] * n_z), _pad_lanes(dt_bias), _pad_lanes(a_log),
      _pad_lanes(d_skip), ssm_norm_w.reshape(1, d_ssm))


def _ssd_decode_kernel(xbc_ref, prev_ref, dt_ref, z_ref, st0_ref, cw_ref, cb_ref, dtb_ref, alog_ref,
                       dskip_ref, normw_ref, y_ref, st_ref, *, dseq, d_ssm, cl):
    gn = N_SSM_GROUPS * SSM_STATE
    prev = prev_ref[...]
    c = prev.shape[1]
    xp = jnp.concatenate([jnp.zeros((SUBLANES - (CONV_WIDTH - 1), c), F32), prev, xbc_ref[...]], axis=0)
    act = _conv_taps(xp, cw_ref[...], cb_ref[...], dseq)
    pad = lambda v: jnp.concatenate([v, jnp.zeros((cl - dseq, v.shape[1]), F32)], axis=0)
    act = pad(act)
    st_ref[...] = st0_ref[...]
    y = _ssd_chunk(act[:, :d_ssm], act[:, d_ssm:d_ssm + gn], act[:, d_ssm + gn:], pad(dt_ref[...]),
                   pad(z_ref[...]), st_ref, dtb_ref[...], alog_ref[...], dskip_ref[...], normw_ref[...],
                   n_valid=dseq)
    y_ref[...] = y[:dseq]


def _ssd_decode(xbc, conv_prev, dt, z_src, z_col0, state0, conv_w, conv_b, dt_bias, a_log, d_skip,
                ssm_norm_w, *, dseq, d_ssm):
    m, c = xbc.shape
    n_b = m // dseq
    n_heads = d_ssm // SSM_HEAD_DIM
    st_rows = n_heads * SSM_HEAD_DIM
    cl = SSD_CHUNK
    tok = lambda b: (b, 0)
    fix = lambda b: (0, 0)
    st = lambda b: (b, 0, 0)
    assert z_col0 % d_ssm == 0
    kern = functools.partial(_ssd_decode_kernel, dseq=dseq, d_ssm=d_ssm, cl=cl)
    return pl.pallas_call(
        kern,
        out_shape=(jax.ShapeDtypeStruct((m, d_ssm), F32),
                   jax.ShapeDtypeStruct((n_b, st_rows, SSM_STATE), F32)),
        grid=(n_b,),
        in_specs=[pl.BlockSpec((dseq, c), tok),
                  pl.BlockSpec((None, CONV_WIDTH - 1, c), st),
                  pl.BlockSpec((dseq, LANES), tok),
                  pl.BlockSpec((dseq, d_ssm), lambda b: (b, z_col0 // d_ssm)),
                  pl.BlockSpec((None, st_rows, SSM_STATE), st),
                  pl.BlockSpec((CONV_WIDTH, c), fix), pl.BlockSpec((1, c), fix),
                  pl.BlockSpec((1, LANES), fix), pl.BlockSpec((1, LANES), fix),
                  pl.BlockSpec((1, LANES), fix), pl.BlockSpec((1, d_ssm), fix)],
        out_specs=(pl.BlockSpec((dseq, d_ssm), tok),
                   pl.BlockSpec((None, st_rows, SSM_STATE), st)),
        compiler_params=_cparams(("parallel",)),
        name="ssd_decode",
    )(xbc, conv_prev, dt, z_src, state0.reshape(n_b, st_rows, SSM_STATE), conv_w, conv_b.reshape(1, c),
      _pad_lanes(dt_bias), _pad_lanes(a_log), _pad_lanes(d_skip), ssm_norm_w.reshape(1, d_ssm))


def _layer(xp, xd, *, bp, seq, bd, dseq, pos_p, pos_d, wts, dec, lam_init):
    d_model = xp.shape[1]
    d_attn = d_model // 2
    d_ssm = d_model - d_attn
    n_heads = d_attn // ATTN_VHEAD
    n_kv = max(1, n_heads // KV_GROUP)
    d_q, d_k, d_v = n_heads * LANES, n_kv * LANES, n_kv * ATTN_VHEAD
    conv_dim = d_ssm + 2 * N_SSM_GROUPS * SSM_STATE
    c_z = d_q + d_k + d_v
    c_x = c_z + d_ssm
    c_dt = c_x + conv_dim
    n_ssm_heads = d_ssm // SSM_HEAD_DIM

    hp, hd = _rmsnorm(xp, wts["ln1_w"]), _rmsnorm(xd, wts["ln1_w"])
    proj_p, proj_d = _matmul_w32([hp], [hd], wts["w_in_t"], n_out=c_dt, w_is_transposed=True,
                                 name="in_proj")
    dt_p = _dt_proj(hp, wts["w_in_t"], c_dt, n_ssm_heads)
    dt_d = _dt_proj(hd, wts["w_in_t"], c_dt, n_ssm_heads)

    tab_p = _rope_tables(pos_p)
    tab_d = tuple(jnp.tile(t, (bd, 1)) for t in _rope_tables(pos_d))
    lam_rows = jnp.stack([wts["lambda_q1"], wts["lambda_k1"], wts["lambda_q2"], wts["lambda_k2"]])
    qk = dict(d_q=d_q, d_k=d_k, d_v=d_v)

    blk = _tile(seq, ATTN_BLOCK, LANES)
    qt, kp_f32, k0, k1, vp_f32, vt = _qkv_post(proj_p, tab_p, wts["q_norm_w"], wts["k_norm_w"],
                                               seq_blocks=seq // blk, **qk)
    o_p = _attn_prompt(qt, k0, k1, vt, lam_rows, wts["subln_w"], lam_init=lam_init)
    act = _conv_prompt(proj_p, c_x, wts["conv_w"], wts["conv_b"], batch=bp, seq=seq)
    y_p, ssm_p = _ssd_prompt(act, dt_p, proj_p, c_z, wts["dt_bias"], wts["a_log"], wts["d_skip"],
                             wts["ssm_norm_w"], batch=bp, seq=seq, d_ssm=d_ssm)
    conv_p = proj_p.reshape(bp, seq, c_dt)[:, seq - (CONV_WIDTH - 1):, c_x:]

    q_d, kd_f32, vd_f32 = _qkv_post(proj_d, tab_d, wts["q_norm_w"], wts["k_norm_w"], **qk)
    o_d = _attn_decode(q_d, kd_f32, vd_f32, dec["cache_k"], dec["cache_v"], dec["page_table"],
                       lam_rows, wts["subln_w"], page=dec["page"], dseq=dseq, lam_init=lam_init)
    xbc_d = proj_d[:, c_x:]
    y_d, ssm_d = _ssd_decode(xbc_d, dec["state_conv"], dt_d, proj_d[:, c_z:c_x], 0, dec["state_ssm"],
                             wts["conv_w"], wts["conv_b"], wts["dt_bias"], wts["a_log"], wts["d_skip"],
                             wts["ssm_norm_w"], dseq=dseq, d_ssm=d_ssm)
    conv_d = jnp.concatenate([dec["state_conv"], xbc_d.reshape(bd, dseq, conv_dim)],
                             axis=1)[:, -(CONV_WIDTH - 1):]

    x1_p, x1_d = _matmul_w32([o_p, y_p], [o_d, y_d], wts["w_out"], n_out=d_model, res=xp, resd=xd,
                             name="out_proj")
    h2_p, h2_d = _rmsnorm(x1_p, wts["ln2_w"]), _rmsnorm(x1_d, wts["ln2_w"])
    up_p, up_d = _matmul_w32([h2_p], [h2_d], wts["w_up"], n_out=wts["w_up"].shape[1], act="relu2",
                             out_dtype=BF16, tm=1024, name="ffn_up")
    d_ff = wts["w_down"].shape[0]
    kc = _tile(d_ff, FFN_DOWN_CHUNK, LANES)
    out_p, out_d = x1_p, x1_d
    for c in range(d_ff // kc):
        out_p, out_d = _matmul_w32([up_p], [up_d], wts["w_down"], n_out=d_model, res=out_p, resd=out_d,
                                   k_chunk=(c, kc), name="ffn_down")

    def heads(a, b, s, w):
        return a.reshape(b, s, n_kv, w)

    state = lambda a, b: a.reshape(b, n_ssm_heads, SSM_HEAD_DIM, SSM_STATE)
    return (out_p, out_d,
            (heads(kp_f32, bp, seq, LANES), heads(vp_f32, bp, seq, ATTN_VHEAD), state(ssm_p, bp), conv_p,
             heads(kd_f32, bd, dseq, LANES), heads(vd_f32, bd, dseq, ATTN_VHEAD), state(ssm_d, bd), conv_d))


def kernel(x_prompt, x_sample, cache_k, cache_v, state_ssm, state_conv, page_table, ln1_w, w_in, q_norm_w, k_norm_w, lambda_q1, lambda_k1, lambda_q2, lambda_k2, subln_w, conv_w, conv_b, dt_bias, a_log, d_skip, ssm_norm_w, w_out, ln2_w, w_up, w_down):
    depth = w_in.shape[0]
    bp, seq, d_model = x_prompt.shape
    bs, dseq, _ = x_sample.shape
    page = cache_k.shape[2]
    past_len = page_table.shape[1] * page
    pos_p = jnp.arange(seq, dtype=jnp.int32)
    pos_s = past_len + jnp.arange(dseq, dtype=jnp.int32)

    yp = x_prompt.reshape(bp * seq, d_model)
    ys = x_sample.reshape(bs * dseq, d_model)
    outs = [[] for _ in range(8)]
    for l in range(depth):
        lam_init = 0.8 - 0.6 * math.exp(-0.3 * l)
        w_in_t = w_in[l].T
        wts = dict(
            ln1_w=ln1_w[l], w_in_t=w_in_t,
            q_norm_w=q_norm_w[l], k_norm_w=k_norm_w[l],
            lambda_q1=lambda_q1[l], lambda_k1=lambda_k1[l], lambda_q2=lambda_q2[l], lambda_k2=lambda_k2[l],
            subln_w=subln_w[l], conv_w=conv_w[l], conv_b=conv_b[l], dt_bias=dt_bias[l], a_log=a_log[l],
            d_skip=d_skip[l], ssm_norm_w=ssm_norm_w[l], w_out=w_out[l],
            ln2_w=ln2_w[l], w_up=w_up[l], w_down=w_down[l])
        dec = dict(cache_k=cache_k[l].reshape(-1, cache_k.shape[-1]), cache_v=cache_v[l].reshape(-1, cache_v.shape[-1]),
                   page=page, page_table=page_table, state_conv=state_conv[l], state_ssm=state_ssm[l])
        yp, ys, layer_outs = _layer(yp, ys, bp=bp, seq=seq, bd=bs, dseq=dseq, pos_p=pos_p, pos_d=pos_s,
                                    wts=wts, dec=dec, lam_init=lam_init)
        for lst, val in zip(outs, layer_outs):
            lst.append(val)
    stacked = [jnp.stack(lst) for lst in outs]
    return (yp.reshape(bp, seq, d_model), ys.reshape(bs, dseq, d_model), *stacked)
```

```python
import functools
import math

import jax
import jax.numpy as jnp
from jax import lax
from jax.experimental import pallas as pl
from jax.experimental.pallas import tpu as pltpu

F32 = jnp.float32
BF16 = jnp.bfloat16

EPS = 1e-6
LANES = 128
SUBLANES = 8
VMEM_LIMIT = 60 * 1024 * 1024

ATTN_VHEAD = 128
ATTN_SUB = 64
ROT_DIM = 16
ROPE_THETA = 500000.0
KV_GROUP = 4
SSM_HEAD_DIM = 64
SSM_STATE = 128
N_SSM_GROUPS = 8
CONV_WIDTH = 4
SSD_CHUNK = 128
FFN_DOWN_CHUNK = 4096
NEG_INF = float("-inf")


def _cparams(sem):
    return pltpu.CompilerParams(dimension_semantics=sem, vmem_limit_bytes=VMEM_LIMIT)


def _tile(n, target, mult=1, also=()):
    t = min(n, target)
    while t > 0:
        if n % t == 0 and t % mult == 0 and all(a % t == 0 for a in also):
            return t
        t -= 1
    raise ValueError((n, target, mult, also))


def _rmsnorm_kernel(x_ref, w_ref, o_ref):
    x = x_ref[...]
    ms = jnp.mean(x * x, axis=-1, keepdims=True)
    o_ref[...] = (x * lax.rsqrt(ms + EPS) * w_ref[...]).astype(o_ref.dtype)


def _rmsnorm(x, w, out_dtype=BF16):
    m, d = x.shape
    tr = _tile(m, 256, SUBLANES)
    return pl.pallas_call(
        _rmsnorm_kernel,
        out_shape=jax.ShapeDtypeStruct((m, d), out_dtype),
        grid=(m // tr,),
        in_specs=[pl.BlockSpec((tr, d), lambda i: (i, 0)),
                  pl.BlockSpec((1, d), lambda i: (0, 0))],
        out_specs=pl.BlockSpec((tr, d), lambda i: (i, 0)),
        compiler_params=_cparams(("parallel",)),
        name="rmsnorm",
    )(x, w.reshape(1, d))


def _mm_kernel(*refs, n_pairs, has_res, nk, act):
    pairs = refs[:2 * n_pairs]
    pos = 2 * n_pairs
    res_ref = refs[pos] if has_res else None
    o_ref = refs[pos + int(has_res)]

    def product():
        part = None
        for p in range(n_pairs):
            x = pairs[2 * p][...]
            if x.dtype != BF16:
                x = x.astype(BF16)
            d = jnp.dot(x, pairs[2 * p + 1][...], preferred_element_type=F32)
            part = d if part is None else part + d
        return part

    if nk == 1:
        part = product()
        if act == "relu2":
            part = jnp.square(jnp.maximum(part, 0.0))
        if has_res:
            part = res_ref[...] + part
        o_ref[...] = part.astype(o_ref.dtype)
    else:
        k = pl.program_id(2)

        @pl.when(k == 0)
        def _():
            o_ref[...] = (res_ref[...] + product()) if has_res else product()

        @pl.when(k > 0)
        def _():
            o_ref[...] += product()


def _matmul(pairs, *, n_out, w_col0=0, res=None, act=None, out_dtype=F32,
            tm=1024, tn=1024, tk=None, name="matmul"):
    m, kdim = pairs[0][0].shape
    tm = _tile(m, tm, SUBLANES)
    tn = _tile(n_out, tn, LANES, also=(w_col0,))
    tk = kdim if tk is None else _tile(kdim, tk, LANES)
    nk = kdim // tk
    if nk > 1:
        assert act is None and out_dtype == F32
    c0 = w_col0 // tn
    in_specs, args = [], []
    for x, w in pairs:
        assert x.shape == (m, kdim) and w.shape[0] == kdim
        in_specs += [pl.BlockSpec((tm, tk), lambda i, j, k: (i, k)),
                     pl.BlockSpec((tk, tn), lambda i, j, k: (k, j + c0))]
        args += [x, w]
    if res is not None:
        in_specs.append(pl.BlockSpec((tm, tn), lambda i, j, k: (i, j)))
        args.append(res)
    kern = functools.partial(_mm_kernel, n_pairs=len(pairs), has_res=res is not None, nk=nk, act=act)
    return pl.pallas_call(
        kern,
        out_shape=jax.ShapeDtypeStruct((m, n_out), out_dtype),
        grid=(m // tm, n_out // tn, nk),
        in_specs=in_specs,
        out_specs=pl.BlockSpec((tm, tn), lambda i, j, k: (i, j)),
        compiler_params=_cparams(("parallel", "parallel", "arbitrary")),
        name=name,
    )(*args)


def _mmw_kernel(*refs, n_x, has_res, act, tn, c0, nj, w_is_transposed, w_row0, kdim):
    xs = refs[:n_x]
    w_any = refs[n_x]
    xds = refs[n_x + 1:2 * n_x + 1]
    pos = 2 * n_x + 1
    res_ref = resd_ref = None
    if has_res:
        res_ref, resd_ref = refs[pos], refs[pos + 1]
        pos += 2
    o_ref, od_ref, wf_scr, wb_scr, sem = refs[pos:pos + 5]
    j, i = pl.program_id(0), pl.program_id(1)

    def w_copy(jj):
        col = pl.multiple_of((jj + c0) * tn, tn)
        rows = pl.ds(w_row0, kdim)
        panel = w_any.at[pl.ds(col, tn), rows] if w_is_transposed else w_any.at[rows, pl.ds(col, tn)]
        return pltpu.make_async_copy(panel, wf_scr, sem)

    def product(x_refs):
        part, k0 = None, 0
        for xr in x_refs:
            x = xr[...]
            if x.dtype != BF16:
                x = x.astype(BF16)
            kp = x.shape[1]
            d = jnp.dot(x, wb_scr[k0:k0 + kp, :], preferred_element_type=F32)
            part = d if part is None else part + d
            k0 += kp
        return part

    def finish(part, r_ref, out_ref):
        if act == "relu2":
            part = jnp.square(jnp.maximum(part, 0.0))
        if r_ref is not None:
            part = r_ref[...] + part
        out_ref[...] = part.astype(out_ref.dtype)

    @pl.when(i == 0)
    def _():
        @pl.when(j == 0)
        def _():
            w_copy(0).start()

        w_copy(j).wait()
        panel = wf_scr[...]
        wb_scr[...] = (panel.T if w_is_transposed else panel).astype(BF16)

        @pl.when(j + 1 < nj)
        def _():
            w_copy(j + 1).start()

        finish(product(xds), resd_ref, od_ref)

    finish(product(xs), res_ref, o_ref)


def _matmul_w32(xs, xds, w, *, n_out, w_col0=0, res=None, resd=None, act=None, out_dtype=F32,
                tm=512, tn=1024, w_is_transposed=False, k_chunk=None, name="matmul_w32"):
    m, md = xs[0].shape[0], xds[0].shape[0]
    if k_chunk is None:
        kdim = w.shape[1] if w_is_transposed else w.shape[0]
        widths, xcb, w_row0 = [x.shape[1] for x in xs], 0, 0
        assert sum(widths) == kdim and widths == [x.shape[1] for x in xds]
    else:
        xcb, kdim = k_chunk
        widths, w_row0 = [kdim], xcb * kdim
        assert len(xs) == 1 and xs[0].shape[1] % kdim == 0
    tm = _tile(m, tm, SUBLANES)
    tn = _tile(n_out, tn, LANES, also=(w_col0,))
    nj = n_out // tn
    in_specs = [pl.BlockSpec((tm, kp), lambda j, i: (i, xcb)) for kp in widths]
    in_specs.append(pl.BlockSpec(memory_space=pl.ANY))
    in_specs += [pl.BlockSpec((md, kp), lambda j, i: (0, xcb)) for kp in widths]
    args = [*xs, w, *xds]
    if res is not None:
        in_specs += [pl.BlockSpec((tm, tn), lambda j, i: (i, j)), pl.BlockSpec((md, tn), lambda j, i: (0, j))]
        args += [res, resd]
    kern = functools.partial(_mmw_kernel, n_x=len(xs), has_res=res is not None, act=act, tn=tn,
                             c0=w_col0 // tn, nj=nj, w_is_transposed=w_is_transposed, w_row0=w_row0,
                             kdim=kdim)
    wf_shape = (tn, kdim) if w_is_transposed else (kdim, tn)
    return pl.pallas_call(
        kern,
        out_shape=(jax.ShapeDtypeStruct((m, n_out), out_dtype), jax.ShapeDtypeStruct((md, n_out), out_dtype)),
        grid=(nj, m // tm),
        in_specs=in_specs,
        out_specs=(pl.BlockSpec((tm, tn), lambda j, i: (i, j)), pl.BlockSpec((md, tn), lambda j, i: (0, j))),
        scratch_shapes=[pltpu.VMEM(wf_shape, F32), pltpu.VMEM((kdim, tn), BF16), pltpu.SemaphoreType.DMA],
        compiler_params=_cparams(("arbitrary", "arbitrary")),
        name=name,
    )(*args)


def _dt_proj_kernel(h_ref, w_ref, o_ref):
    n_dt = w_ref.shape[0]
    dt = _dot_nt(h_ref[...], w_ref[...].astype(BF16))
    o_ref[...] = jnp.concatenate([dt, jnp.zeros((dt.shape[0], LANES - n_dt), F32)], axis=1)


def _dt_proj(h, w_t, row0, n_dt):
    m, kdim = h.shape
    tm = _tile(m, 1024, SUBLANES)
    assert row0 % n_dt == 0 and n_dt % SUBLANES == 0
    return pl.pallas_call(
        _dt_proj_kernel,
        out_shape=jax.ShapeDtypeStruct((m, LANES), F32),
        grid=(m // tm,),
        in_specs=[pl.BlockSpec((tm, kdim), lambda i: (i, 0)),
                  pl.BlockSpec((n_dt, kdim), lambda i: (row0 // n_dt, 0))],
        out_specs=pl.BlockSpec((tm, LANES), lambda i: (i, 0)),
        compiler_params=_cparams(("parallel",)),
        name="in_proj_dt",
    )(h, w_t)


def _norm_rope(x, w, cos, sa, sb):
    x2 = x * x
    x2_hi = x2.astype(BF16)
    x2_lo = (x2 - x2_hi.astype(F32)).astype(BF16)
    same = (lax.broadcasted_iota(jnp.int32, (LANES, LANES), 0) // ATTN_SUB
            == lax.broadcasted_iota(jnp.int32, (LANES, LANES), 1) // ATTN_SUB)
    seg = jnp.where(same, 1.0, 0.0).astype(BF16)
    ssq = (jnp.dot(x2_hi, seg, preferred_element_type=F32) + jnp.dot(x2_lo, seg, preferred_element_type=F32))
    ms = ssq * (1.0 / ATTN_SUB)
    y = x * lax.rsqrt(ms + EPS) * w
    up = pltpu.roll(y, LANES - ROT_DIM // 2, 1)
    dn = pltpu.roll(y, ROT_DIM // 2, 1)
    return y * cos + up * sa + dn * sb


def _qkv_post_decode_kernel(qkv_ref, cos_ref, sa_ref, sb_ref, qw_ref, kw_ref, q_ref, kf_ref, vf_ref,
                            *, d_q, d_k):
    cos, sa, sb = cos_ref[...], sa_ref[...], sb_ref[...]
    qw, kw = qw_ref[...], kw_ref[...]
    scale = ATTN_SUB ** -0.5
    for g in range(d_q // LANES):
        sl = slice(g * LANES, (g + 1) * LANES)
        q_ref[:, sl] = _norm_rope(qkv_ref[:, sl], qw, cos, sa, sb) * scale
    for g in range(d_k // LANES):
        sl = slice(g * LANES, (g + 1) * LANES)
        kf_ref[:, sl] = _norm_rope(qkv_ref[:, d_q + g * LANES:d_q + (g + 1) * LANES], kw, cos, sa, sb)
    vf_ref[...] = qkv_ref[:, d_q + d_k:]


def _qkv_post_prompt_kernel(qkv_ref, cos_ref, sa_ref, sb_ref, qw_ref, kw_ref,
                            qt_ref, kf_ref, k0_ref, k1_ref, vf_ref, vt_ref, *, d_q, d_k):
    tr = qkv_ref.shape[0]
    cos, sa, sb = cos_ref[...], sa_ref[...], sb_ref[...]
    lo = lax.broadcasted_iota(jnp.int32, (tr, LANES), 1) < ATTN_SUB
    qw, kw = qw_ref[...], kw_ref[...]
    scale = (ATTN_SUB ** -0.5) * math.log2(math.e)
    for hd in range(d_q // LANES):
        y = _norm_rope(qkv_ref[:, hd * LANES:(hd + 1) * LANES], qw, cos, sa, sb) * scale
        qt_ref[hd] = y.T.astype(BF16)
    for g in range(d_k // LANES):
        y = _norm_rope(qkv_ref[:, d_q + g * LANES:d_q + (g + 1) * LANES], kw, cos, sa, sb)
        sl = slice(g * LANES, (g + 1) * LANES)
        kf_ref[:, sl] = y
        k0_ref[:, sl] = jnp.where(lo, y, 0.0).astype(BF16)
        k1_ref[:, sl] = jnp.where(lo, 0.0, y).astype(BF16)
    v = qkv_ref[:, d_q + d_k:]
    vf_ref[...] = v
    for g in range(v.shape[1] // LANES):
        vt_ref[g] = v[:, g * LANES:(g + 1) * LANES].T.astype(BF16)


def _rope_tables(pos):
    half = ROT_DIM // 2
    inv = jnp.exp(-math.log(ROPE_THETA) * jnp.arange(half, dtype=F32) * 2.0 / ROT_DIM)
    ang = pos.astype(F32)[:, None] * inv[None, :]
    cos, sin = jnp.cos(ang), jnp.sin(ang)
    n = pos.shape[0]
    ones = jnp.ones((n, ATTN_SUB - ROT_DIM), F32)
    zeros8 = jnp.zeros((n, half), F32)
    zeros = jnp.zeros((n, ATTN_SUB - ROT_DIM), F32)
    c = jnp.concatenate([cos, cos, ones], axis=1)
    sa = jnp.concatenate([-sin, zeros8, zeros], axis=1)
    sb = jnp.concatenate([zeros8, sin, zeros], axis=1)
    rep = LANES // ATTN_SUB
    return jnp.tile(c, (1, rep)), jnp.tile(sa, (1, rep)), jnp.tile(sb, (1, rep))


def _qkv_post(qkv, tables, q_norm_w, k_norm_w, *, d_q, d_k, d_v, seq_blocks=None):
    m = qkv.shape[0]
    n_tab = tables[0].shape[0]
    rep = LANES // ATTN_SUB
    row = lambda i: (i, 0)
    fix = lambda i: (0, 0)
    if seq_blocks is None:
        tr = _tile(m, 256, SUBLANES, also=(n_tab,))
    else:
        tr = n_tab // seq_blocks
    ntb = n_tab // tr
    tab = lambda i: (i % ntb, 0)
    in_specs = [pl.BlockSpec((tr, d_q + d_k + d_v), row),
                pl.BlockSpec((tr, LANES), tab), pl.BlockSpec((tr, LANES), tab),
                pl.BlockSpec((tr, LANES), tab),
                pl.BlockSpec((1, LANES), fix), pl.BlockSpec((1, LANES), fix)]
    args = (qkv, *tables, jnp.tile(q_norm_w, rep).reshape(1, LANES), jnp.tile(k_norm_w, rep).reshape(1, LANES))
    if seq_blocks is None:
        return pl.pallas_call(
            functools.partial(_qkv_post_decode_kernel, d_q=d_q, d_k=d_k),
            out_shape=(jax.ShapeDtypeStruct((m, d_q), F32), jax.ShapeDtypeStruct((m, d_k), F32),
                       jax.ShapeDtypeStruct((m, d_v), F32)),
            grid=(m // tr,),
            in_specs=in_specs,
            out_specs=(pl.BlockSpec((tr, d_q), row), pl.BlockSpec((tr, d_k), row),
                       pl.BlockSpec((tr, d_v), row)),
            compiler_params=_cparams(("parallel",)),
            name="qkv_post_decode",
        )(*args)
    batch = m // n_tab
    n_heads, n_kv = d_q // LANES, d_v // LANES
    tmap = lambda i: (i // seq_blocks, 0, i % seq_blocks, 0, 0)
    return pl.pallas_call(
        functools.partial(_qkv_post_prompt_kernel, d_q=d_q, d_k=d_k),
        out_shape=(jax.ShapeDtypeStruct((batch, n_heads, seq_blocks, LANES, tr), BF16),
                   jax.ShapeDtypeStruct((m, d_k), F32),
                   jax.ShapeDtypeStruct((m, d_k), BF16), jax.ShapeDtypeStruct((m, d_k), BF16),
                   jax.ShapeDtypeStruct((m, d_v), F32),
                   jax.ShapeDtypeStruct((batch, n_kv, seq_blocks, LANES, tr), BF16)),
        grid=(m // tr,),
        in_specs=in_specs,
        out_specs=(pl.BlockSpec((None, n_heads, None, LANES, tr), tmap),
                   pl.BlockSpec((tr, d_k), row), pl.BlockSpec((tr, d_k), row), pl.BlockSpec((tr, d_k), row),
                   pl.BlockSpec((tr, d_v), row),
                   pl.BlockSpec((None, n_kv, None, LANES, tr), tmap)),
        compiler_params=_cparams(("parallel",)),
        name="qkv_post_prompt",
    )(*args)


def _lambda_value(lam_ref, lam_init):
    l = lam_ref[...]
    d1 = jnp.sum(l[0:1] * l[1:2], axis=-1, keepdims=True)
    d2 = jnp.sum(l[2:3] * l[3:4], axis=-1, keepdims=True)
    return jnp.exp(d1) - jnp.exp(d2) + lam_init


def _subln(o, w, lam_init):
    ms = jnp.mean(o * o, axis=-1, keepdims=True)
    return (o * lax.rsqrt(ms + EPS) * w) * (1.0 - lam_init)


def _dot_nt(a, b):
    return lax.dot_general(a, b, (((1,), (1,)), ((), ())), preferred_element_type=F32)


ONES_ROWS = 16
ATTN_BLOCK = 256
ATTN_SPAN = 3


def _attn_prompt_kernel(qt_ref, k0_ref, k1_ref, vt_ref, lam_ref, subw_ref, o_ref, acc_scr,
                        *, blk, lam_init):
    i = pl.program_id(2)
    acc_scr[...] = jnp.zeros(acc_scr.shape, F32)
    k_refs = (k0_ref, k1_ref)

    def kv_span(j, n_blk, m_all, diagonal):
        nkeys = n_blk * blk
        v_blocks = jnp.concatenate([vt_ref[j + t] for t in range(n_blk)], axis=1)
        vt = jnp.concatenate([v_blocks, jnp.ones((ONES_ROWS, nkeys), BF16)], axis=0)
        start = pl.multiple_of(j * blk, blk)
        if diagonal:
            visible = (lax.broadcasted_iota(jnp.int32, (nkeys, blk), 0) - (nkeys - blk)
                       <= lax.broadcasted_iota(jnp.int32, (nkeys, blk), 1))
        kcs = [k_refs[c][pl.ds(start, nkeys), :] for c in range(2)]
        hcs = [(h, c) for h in range(KV_GROUP) for c in range(2)]
        scores = [jnp.dot(kcs[c], qt_ref[h], preferred_element_type=F32) for h, c in hcs]
        m_rows, alphas, probs = [], [], []
        for idx, s in enumerate(scores):
            if diagonal:
                s = jnp.where(visible, s, NEG_INF)
            m_old = m_all[idx:idx + 1, :]
            m_new = jnp.maximum(m_old, jnp.max(s, axis=0, keepdims=True))
            probs.append(jnp.exp2(s - m_new).astype(BF16))
            alphas.append(jnp.exp2(m_old - m_new))
            m_rows.append(m_new)
        for idx, p in enumerate(probs):
            pv = jnp.dot(vt, p, preferred_element_type=F32)
            acc_scr[idx] = alphas[idx] * acc_scr[idx] + pv
        return jnp.concatenate(m_rows, axis=0)

    m_init = jnp.full((2 * KV_GROUP, blk), NEG_INF, F32)
    m_all = lax.fori_loop(0, i // ATTN_SPAN,
                          lambda js, m: kv_span(ATTN_SPAN * js, ATTN_SPAN, m, False), m_init)
    for tail in range(ATTN_SPAN):
        @pl.when(i % ATTN_SPAN == tail)
        def _(tail=tail):
            kv_span(i - tail, tail + 1, m_all, True)

    lam = _lambda_value(lam_ref, lam_init)
    w = subw_ref[...]
    for h in range(KV_GROUP):
        a0, a1 = acc_scr[2 * h], acc_scr[2 * h + 1]
        o0 = a0[:LANES] / a0[LANES:LANES + 1]
        o1 = a1[:LANES] / a1[LANES:LANES + 1]
        o = _subln((o0 - lam * o1).T, w, lam_init)
        o_ref[:, h * LANES:(h + 1) * LANES] = o.astype(o_ref.dtype)


def _attn_prompt(qt, k0, k1, vt, lam_rows, subln_w, *, lam_init):
    batch, n_heads, nb, _, blk = qt.shape
    n_kv = vt.shape[1]
    seq = nb * blk
    gw = KV_GROUP * LANES
    fix = lambda b, g, i: (0, 0)
    kern = functools.partial(_attn_prompt_kernel, blk=blk, lam_init=lam_init)
    return pl.pallas_call(
        kern,
        out_shape=jax.ShapeDtypeStruct((batch * seq, n_heads * LANES), BF16),
        grid=(batch, n_kv, nb),
        in_specs=[pl.BlockSpec((None, KV_GROUP, None, LANES, blk), lambda b, g, i: (b, g, i, 0, 0)),
                  pl.BlockSpec((seq, LANES), lambda b, g, i: (b, g)),
                  pl.BlockSpec((seq, LANES), lambda b, g, i: (b, g)),
                  pl.BlockSpec((None, None, nb, LANES, blk), lambda b, g, i: (b, g, 0, 0, 0)),
                  pl.BlockSpec((4, ATTN_SUB), fix),
                  pl.BlockSpec((1, LANES), fix)],
        out_specs=pl.BlockSpec((blk, gw), lambda b, g, i: (b * nb + i, g)),
        scratch_shapes=[pltpu.VMEM((2 * KV_GROUP, LANES + ONES_ROWS, blk), F32)],
        compiler_params=_cparams(("parallel", "parallel", "parallel")),
        name="attn_prompt",
    )(qt, k0, k1, vt, lam_rows, subln_w.reshape(1, LANES))


DECODE_SLOTS = 3


def _attn_decode_kernel(pt_ref, q_ref, kn_ref, vn_ref, lam_ref, subw_ref, ck_any, cv_any, o_ref,
                        qf_scr, qb_scr, kbuf, vbuf, kc_scr, vc_scr, m_scr, l_scr, acc_scr, sem,
                        *, pps, n_steps, n_b, page, dseq, n_kv, lam_init):
    b, j = pl.program_id(0), pl.program_id(1)
    rows = qf_scr.shape[0]
    blk = 2 * dseq
    prow = page * n_kv
    n_chunks = n_b * n_steps

    def chunk_copies(c):
        cb, cj, slot = c // n_steps, c % n_steps, c % DECODE_SLOTS
        copies = []
        for t in range(pps):
            src = pl.ds(pl.multiple_of(pt_ref[cb, cj * pps + t] * prow, prow), prow)
            dst = pl.ds(t * prow, prow)
            copies.append(pltpu.make_async_copy(ck_any.at[src, :], kbuf.at[slot, dst, :], sem.at[0, slot]))
            copies.append(pltpu.make_async_copy(cv_any.at[src, :], vbuf.at[slot, dst, :], sem.at[1, slot]))
        return copies

    @pl.when(j == 0)
    def _():
        qf_scr[...] = jnp.zeros(qf_scr.shape, F32)
        lo = lax.broadcasted_iota(jnp.int32, (dseq, LANES), 1) < ATTN_SUB
        for g in range(n_kv):
            for h in range(KV_GROUP):
                hd = g * KV_GROUP + h
                qh = q_ref[:, hd * LANES:(hd + 1) * LANES]
                r0 = hd * blk
                qf_scr[r0:r0 + dseq, g * LANES:(g + 1) * LANES] = jnp.where(lo, qh, 0.0)
                qf_scr[r0 + dseq:r0 + blk, g * LANES:(g + 1) * LANES] = jnp.where(lo, 0.0, qh)
        qb_scr[...] = qf_scr[...].astype(BF16)
        m_scr[...] = jnp.full(m_scr.shape, NEG_INF, F32)
        l_scr[...] = jnp.zeros(l_scr.shape, F32)
        acc_scr[...] = jnp.zeros(acc_scr.shape, F32)

    n_grp = max(1, n_kv // 2)
    gr, gc = rows // n_grp, kc_scr.shape[1] // n_grp

    def online_update(k, v, mask=None):
        s = jnp.concatenate([_dot_nt(qb_scr[t * gr:(t + 1) * gr, t * gc:(t + 1) * gc],
                                     k[:, t * gc:(t + 1) * gc]) for t in range(n_grp)], axis=0)
        if mask is not None:
            s = jnp.where(mask, s, NEG_INF)
        m_old = m_scr[...]
        m_new = jnp.maximum(m_old, jnp.max(s, axis=-1, keepdims=True))
        p = jnp.exp(s - jnp.tile(m_new, (1, s.shape[1] // LANES)))
        alpha = jnp.exp(m_old - m_new)
        l_scr[...] = alpha * l_scr[...] + jnp.sum(p, axis=-1, keepdims=True)
        p16 = p.astype(BF16)
        for t in range(n_grp):
            rs, cs = slice(t * gr, (t + 1) * gr), slice(t * gc, (t + 1) * gc)
            pv = jnp.dot(p16[rs], v[:, cs], preferred_element_type=F32)
            acc_scr[rs, cs] = jnp.tile(alpha[rs], (1, gc // LANES)) * acc_scr[rs, cs] + pv
        m_scr[...] = m_new

    @pl.when(j < n_steps)
    def _():
        c = b * n_steps + j

        @pl.when(c == 0)
        def _():
            for ahead in range(min(DECODE_SLOTS - 1, n_chunks)):
                for cp in chunk_copies(ahead):
                    cp.start()

        @pl.when(c + DECODE_SLOTS - 1 < n_chunks)
        def _():
            for cp in chunk_copies(c + DECODE_SLOTS - 1):
                cp.start()

        for cp in chunk_copies(c):
            cp.wait()
        slot = c % DECODE_SLOTS
        for t in range(pps):
            for g in range(n_kv):
                head_rows = pl.ds(t * prow + g, page, stride=n_kv)
                cols = slice(g * LANES, (g + 1) * LANES)
                kc_scr[t * page:(t + 1) * page, cols] = kbuf[slot, head_rows, :].astype(BF16)
                vc_scr[t * page:(t + 1) * page, cols] = vbuf[slot, head_rows, :].astype(BF16)
        online_update(kc_scr, vc_scr)

    @pl.when(j == n_steps)
    def _():
        kpad = jnp.concatenate([kn_ref[...], jnp.zeros((page - dseq, kn_ref.shape[1]), F32)], axis=0)
        vpad = jnp.concatenate([vn_ref[...], jnp.zeros((page - dseq, vn_ref.shape[1]), F32)], axis=0)
        t_idx = lax.broadcasted_iota(jnp.int32, (rows, page), 0) % dseq
        u_idx = lax.broadcasted_iota(jnp.int32, (rows, page), 1)
        online_update(kpad.astype(BF16), vpad.astype(BF16), mask=u_idx <= t_idx)

        lam = _lambda_value(lam_ref, lam_init)
        w = subw_ref[...]
        for g in range(n_kv):
            for h in range(KV_GROUP):
                hd = g * KV_GROUP + h
                r0 = hd * blk
                cs = slice(g * LANES, (g + 1) * LANES)
                o0 = acc_scr[r0:r0 + dseq, cs] / l_scr[r0:r0 + dseq, :]
                o1 = acc_scr[r0 + dseq:r0 + blk, cs] / l_scr[r0 + dseq:r0 + blk, :]
                o_ref[:, hd * LANES:(hd + 1) * LANES] = _subln(o0 - lam * o1, w, lam_init)


def _attn_decode(q, k_new, v_new, cache_k, cache_v, page_table, lam_rows, subln_w, *, page, dseq, lam_init):
    m, d_q = q.shape
    n_b, n_pages = page_table.shape
    kvw = k_new.shape[1]
    n_kv = kvw // LANES
    pps = _tile(n_pages, 8)
    n_steps = n_pages // pps
    rows = n_kv * KV_GROUP * 2 * dseq

    tok = lambda b, j, pt: (b, 0)
    fix = lambda b, j, pt: (0, 0)
    chunk_rows = pps * page * n_kv
    kern = functools.partial(_attn_decode_kernel, pps=pps, n_steps=n_steps, n_b=n_b, page=page, dseq=dseq,
                             n_kv=n_kv, lam_init=lam_init)
    return pl.pallas_call(
        kern,
        out_shape=jax.ShapeDtypeStruct((m, d_q), F32),
        grid_spec=pltpu.PrefetchScalarGridSpec(
            num_scalar_prefetch=1,
            grid=(n_b, n_steps + 1),
            in_specs=[pl.BlockSpec((dseq, d_q), tok),
                      pl.BlockSpec((dseq, kvw), tok), pl.BlockSpec((dseq, kvw), tok),
                      pl.BlockSpec((4, ATTN_SUB), fix), pl.BlockSpec((1, LANES), fix),
                      pl.BlockSpec(memory_space=pl.ANY), pl.BlockSpec(memory_space=pl.ANY)],
            out_specs=pl.BlockSpec((dseq, d_q), tok),
            scratch_shapes=[pltpu.VMEM((rows, kvw), F32), pltpu.VMEM((rows, kvw), BF16),
                            pltpu.VMEM((DECODE_SLOTS, chunk_rows, LANES), F32),
                            pltpu.VMEM((DECODE_SLOTS, chunk_rows, LANES), F32),
                            pltpu.VMEM((pps * page, kvw), BF16), pltpu.VMEM((pps * page, kvw), BF16),
                            pltpu.VMEM((rows, LANES), F32), pltpu.VMEM((rows, LANES), F32),
                            pltpu.VMEM((rows, kvw), F32),
                            pltpu.SemaphoreType.DMA((2, DECODE_SLOTS))]),
        compiler_params=_cparams(("arbitrary", "arbitrary")),
        name="attn_decode",
    )(page_table, q, k_new, v_new, lam_rows, subln_w.reshape(1, LANES), cache_k, cache_v)


def _silu(x):
    return x * (1.0 / (1.0 + jnp.exp(-x)))


def _conv_taps(xp, w, bias, n_rows):
    acc = None
    for t in range(CONV_WIDTH):
        shift = CONV_WIDTH - 1 - t
        xs = xp if shift == 0 else pltpu.roll(xp, shift, 0)
        term = xs[SUBLANES:] * w[t:t + 1]
        acc = term if acc is None else acc + term
    return _silu(acc + bias)


def _conv_prompt_kernel(x_ref, halo_ref, w_ref, b_ref, o_ref):
    i = pl.program_id(1)
    halo = jnp.where(i > 0, halo_ref[...], 0.0)
    xp = jnp.concatenate([halo, x_ref[...]], axis=0)
    o_ref[...] = _conv_taps(xp, w_ref[...], b_ref[...], x_ref.shape[0])


def _conv_prompt(src, col0, conv_w, conv_b, *, batch, seq):
    m = src.shape[0]
    c = conv_w.shape[1]
    tr = _tile(seq, 256, SUBLANES)
    tc = _tile(c, 1024, LANES, also=(col0,))
    nr = seq // tr
    hb = tr // SUBLANES
    cb0 = col0 // tc
    return pl.pallas_call(
        _conv_prompt_kernel,
        out_shape=jax.ShapeDtypeStruct((m, c), F32),
        grid=(batch, nr, c // tc),
        in_specs=[pl.BlockSpec((tr, tc), lambda b, i, j: (b * nr + i, j + cb0)),
                  pl.BlockSpec((SUBLANES, tc),
                               lambda b, i, j: (jnp.maximum((b * nr + i) * hb - 1, 0), j + cb0)),
                  pl.BlockSpec((CONV_WIDTH, tc), lambda b, i, j: (0, j)),
                  pl.BlockSpec((1, tc), lambda b, i, j: (0, j))],
        out_specs=pl.BlockSpec((tr, tc), lambda b, i, j: (b * nr + i, j)),
        compiler_params=_cparams(("parallel", "parallel", "parallel")),
        name="conv_prompt",
    )(src, src, conv_w, conv_b.reshape(1, c))


def _cumsum_rows(x):
    n = x.shape[0]
    row = lax.broadcasted_iota(jnp.int32, x.shape, 0)
    s = 1
    while s < n:
        x = x + jnp.where(row >= s, pltpu.roll(x, s, 0), 0.0)
        s *= 2
    return x


def _ssd_chunk(xs, bm, cm, dt_raw, z, state_ref, dtb, alog, dskip, normw, *, n_valid):
    cl, d_ssm = xs.shape
    n_heads = d_ssm // SSM_HEAD_DIM
    hpg = n_heads // N_SSM_GROUPS
    gs = d_ssm // N_SSM_GROUPS
    x_dt = dt_raw + dtb
    dtp = jnp.maximum(x_dt, 0.0) + jnp.log1p(jnp.exp(-jnp.abs(x_dt)))
    if n_valid < cl:
        dtp = jnp.where(lax.broadcasted_iota(jnp.int32, dtp.shape, 0) < n_valid, dtp, 0.0)
    a = -jnp.exp(alog)
    cs = _cumsum_rows(dtp * a)
    cs_t = cs.T
    cs_last = cs[cl - 1:cl, :]
    dec_all = jnp.exp(cs_last)
    tri = lax.broadcasted_iota(jnp.int32, (cl, cl), 0) >= lax.broadcasted_iota(jnp.int32, (cl, cl), 1)
    lo = lax.broadcasted_iota(jnp.int32, (cl, LANES), 1) < SSM_HEAD_DIM
    lo_row = lo[0:1]
    hsel = lax.broadcasted_iota(jnp.int32, (2 * SSM_HEAD_DIM, SSM_STATE), 0) < SSM_HEAD_DIM
    n_pairs = n_heads // 2
    group_of = lambda pr: (2 * pr) // hpg

    b16 = [bm[:, g * SSM_STATE:(g + 1) * SSM_STATE].astype(BF16) for g in range(N_SSM_GROUPS)]
    c16 = [cm[:, g * SSM_STATE:(g + 1) * SSM_STATE].astype(BF16) for g in range(N_SSM_GROUPS)]
    cb = [_dot_nt(c16[g], b16[g]) for g in range(N_SSM_GROUPS)]
    states = [state_ref[pr * LANES:(pr + 1) * LANES, :] for pr in range(n_pairs)]
    y_off_raw = [_dot_nt(c16[group_of(pr)], states[pr].astype(BF16)) for pr in range(n_pairs)]

    x_pairs, xdt16, xdec16, scores, dec_in = [], [], [], [], []
    for pr in range(n_pairs):
        h0 = 2 * pr
        full = [jnp.broadcast_to(cs[:, h:h + 1], (cl, LANES)) for h in (h0, h0 + 1)]
        cs_pair = jnp.where(lo, full[0], full[1])
        dt_pair = jnp.where(lo, dtp[:, h0:h0 + 1], dtp[:, h0 + 1:h0 + 2])
        last_pair = jnp.where(lo_row, cs_last[:, h0:h0 + 1], cs_last[:, h0 + 1:h0 + 2])
        x_pair = xs[:, pr * LANES:(pr + 1) * LANES]
        xdt = x_pair * dt_pair
        x_pairs.append(x_pair)
        xdt16.append(xdt.astype(BF16))
        xdec16.append((xdt * jnp.exp(last_pair - cs_pair)).astype(BF16))
        dec_in.append(jnp.exp(cs_pair))
        for e in range(2):
            seg = full[e] - cs_t[h0 + e:h0 + e + 1, :]
            lmat = jnp.exp(jnp.where(tri, seg, NEG_INF))
            scores.append((cb[group_of(pr)] * lmat).astype(BF16))

    y_parts = []
    for pr in range(n_pairs):
        h0 = 2 * pr
        halves = [jnp.dot(scores[h0 + e], xdt16[pr], preferred_element_type=F32) for e in range(2)]
        y_diag = jnp.where(lo, halves[0], halves[1])
        skip = jnp.where(lo_row, dskip[:, h0:h0 + 1], dskip[:, h0 + 1:h0 + 2])
        y_parts.append(y_diag + y_off_raw[pr] * dec_in[pr] + skip * x_pairs[pr])
        new = lax.dot_general(xdec16[pr], b16[group_of(pr)], (((0,), (0,)), ((), ())),
                              preferred_element_type=F32)
        dec = jnp.where(hsel, dec_all[:, h0:h0 + 1], dec_all[:, h0 + 1:h0 + 2])
        state_ref[pr * LANES:(pr + 1) * LANES, :] = dec * states[pr] + new
    y = jnp.concatenate(y_parts, axis=1) * _silu(z)
    outs = []
    for g in range(N_SSM_GROUPS):
        yg = y[:, g * gs:(g + 1) * gs]
        ms = jnp.mean(yg * yg, axis=-1, keepdims=True)
        outs.append(yg * lax.rsqrt(ms + EPS) * normw[:, g * gs:(g + 1) * gs])
    return jnp.concatenate(outs, axis=1)


def _ssd_prompt_kernel(xs_ref, b_ref, c_ref, dt_ref, *rest, n_z):
    z_refs = rest[:n_z]
    dtb_ref, alog_ref, dskip_ref, normw_ref, y_ref, st_ref = rest[n_z:]

    @pl.when(pl.program_id(1) == 0)
    def _():
        st_ref[...] = jnp.zeros(st_ref.shape, F32)

    z = jnp.concatenate([r[...] for r in z_refs], axis=1)
    y = _ssd_chunk(xs_ref[...], b_ref[...], c_ref[...], dt_ref[...], z, st_ref,
                   dtb_ref[...], alog_ref[...], dskip_ref[...], normw_ref[...], n_valid=xs_ref.shape[0])
    y_ref[...] = y.astype(y_ref.dtype)


def _pad_lanes(v):
    return jnp.pad(v, (0, LANES - v.shape[0])).reshape(1, LANES)


def _ssd_prompt(xbc_act, dt, z_src, z_col0, dt_bias, a_log, d_skip, ssm_norm_w, *, batch, seq, d_ssm):
    m = xbc_act.shape[0]
    cl = min(SSD_CHUNK, seq)
    nc = seq // cl
    gn = N_SSM_GROUPS * SSM_STATE
    n_heads = d_ssm // SSM_HEAD_DIM
    row = lambda b, c: (b * nc + c, 0)
    fix = lambda b, c: (0, 0)
    assert d_ssm % gn == 0
    zw = _tile(d_ssm, d_ssm, LANES, also=(z_col0,))
    n_z = d_ssm // zw

    def z_map(t):
        return lambda b, c: (b * nc + c, z_col0 // zw + t)

    return pl.pallas_call(
        functools.partial(_ssd_prompt_kernel, n_z=n_z),
        out_shape=(jax.ShapeDtypeStruct((m, d_ssm), BF16),
                   jax.ShapeDtypeStruct((batch, n_heads * SSM_HEAD_DIM, SSM_STATE), F32)),
        grid=(batch, nc),
        in_specs=[pl.BlockSpec((cl, d_ssm), row),
                  pl.BlockSpec((cl, gn), lambda b, c: (b * nc + c, d_ssm // gn)),
                  pl.BlockSpec((cl, gn), lambda b, c: (b * nc + c, d_ssm // gn + 1)),
                  pl.BlockSpec((cl, LANES), row)]
                 + [pl.BlockSpec((cl, zw), z_map(t)) for t in range(n_z)]
                 + [pl.BlockSpec((1, LANES), fix), pl.BlockSpec((1, LANES), fix),
                    pl.BlockSpec((1, LANES), fix), pl.BlockSpec((1, d_ssm), fix)],
        out_specs=(pl.BlockSpec((cl, d_ssm), row),
                   pl.BlockSpec((None, n_heads * SSM_HEAD_DIM, SSM_STATE), lambda b, c: (b, 0, 0))),
        compiler_params=_cparams(("parallel", "arbitrary")),
        name="ssd_prompt",
    )(xbc_act, xbc_act, xbc_act, dt, *([z_src] * n_z), _pad_lanes(dt_bias), _pad_lanes(a_log),
      _pad_lanes(d_skip), ssm_norm_w.reshape(1, d_ssm))


def _ssd_decode_kernel(xbc_ref, prev_ref, dt_ref, z_ref, st0_ref, cw_ref, cb_ref, dtb_ref, alog_ref,
                       dskip_ref, normw_ref, y_ref, st_ref, *, dseq, d_ssm, cl):
    gn = N_SSM_GROUPS * SSM_STATE
    prev = prev_ref[...]
    c = prev.shape[1]
    xp = jnp.concatenate([jnp.zeros((SUBLANES - (CONV_WIDTH - 1), c), F32), prev, xbc_ref[...]], axis=0)
    act = _conv_taps(xp, cw_ref[...], cb_ref[...], dseq)
    pad = lambda v: jnp.concatenate([v, jnp.zeros((cl - dseq, v.shape[1]), F32)], axis=0)
    act = pad(act)
    st_ref[...] = st0_ref[...]
    y = _ssd_chunk(act[:, :d_ssm], act[:, d_ssm:d_ssm + gn], act[:, d_ssm + gn:], pad(dt_ref[...]),
                   pad(z_ref[...]), st_ref, dtb_ref[...], alog_ref[...], dskip_ref[...], normw_ref[...],
                   n_valid=dseq)
    y_ref[...] = y[:dseq]


def _ssd_decode(xbc, conv_prev, dt, z_src, z_col0, state0, conv_w, conv_b, dt_bias, a_log, d_skip,
                ssm_norm_w, *, dseq, d_ssm):
    m, c = xbc.shape
    n_b = m // dseq
    n_heads = d_ssm // SSM_HEAD_DIM
    st_rows = n_heads * SSM_HEAD_DIM
    cl = SSD_CHUNK
    tok = lambda b: (b, 0)
    fix = lambda b: (0, 0)
    st = lambda b: (b, 0, 0)
    assert z_col0 % d_ssm == 0
    kern = functools.partial(_ssd_decode_kernel, dseq=dseq, d_ssm=d_ssm, cl=cl)
    return pl.pallas_call(
        kern,
        out_shape=(jax.ShapeDtypeStruct((m, d_ssm), F32),
                   jax.ShapeDtypeStruct((n_b, st_rows, SSM_STATE), F32)),
        grid=(n_b,),
        in_specs=[pl.BlockSpec((dseq, c), tok),
                  pl.BlockSpec((None, CONV_WIDTH - 1, c), st),
                  pl.BlockSpec((dseq, LANES), tok),
                  pl.BlockSpec((dseq, d_ssm), lambda b: (b, z_col0 // d_ssm)),
                  pl.BlockSpec((None, st_rows, SSM_STATE), st),
                  pl.BlockSpec((CONV_WIDTH, c), fix), pl.BlockSpec((1, c), fix),
                  pl.BlockSpec((1, LANES), fix), pl.BlockSpec((1, LANES), fix),
                  pl.BlockSpec((1, LANES), fix), pl.BlockSpec((1, d_ssm), fix)],
        out_specs=(pl.BlockSpec((dseq, d_ssm), tok),
                   pl.BlockSpec((None, st_rows, SSM_STATE), st)),
        compiler_params=_cparams(("parallel",)),
        name="ssd_decode",
    )(xbc, conv_prev, dt, z_src, state0.reshape(n_b, st_rows, SSM_STATE), conv_w, conv_b.reshape(1, c),
      _pad_lanes(dt_bias), _pad_lanes(a_log), _pad_lanes(d_skip), ssm_norm_w.reshape(1, d_ssm))


def _layer(xp, xd, *, bp, seq, bd, dseq, pos_p, pos_d, wts, dec, lam_init):
    d_model = xp.shape[1]
    d_attn = d_model // 2
    d_ssm = d_model - d_attn
    n_heads = d_attn // ATTN_VHEAD
    n_kv = max(1, n_heads // KV_GROUP)
    d_q, d_k, d_v = n_heads * LANES, n_kv * LANES, n_kv * ATTN_VHEAD
    conv_dim = d_ssm + 2 * N_SSM_GROUPS * SSM_STATE
    c_z = d_q + d_k + d_v
    c_x = c_z + d_ssm
    c_dt = c_x + conv_dim
    n_ssm_heads = d_ssm // SSM_HEAD_DIM

    hp, hd = _rmsnorm(xp, wts["ln1_w"]), _rmsnorm(xd, wts["ln1_w"])
    proj_p, proj_d = _matmul_w32([hp], [hd], wts["w_in_t"], n_out=c_dt, w_is_transposed=True,
                                 name="in_proj")
    dt_p = _dt_proj(hp, wts["w_in_t"], c_dt, n_ssm_heads)
    dt_d = _dt_proj(hd, wts["w_in_t"], c_dt, n_ssm_heads)

    tab_p = _rope_tables(pos_p)
    tab_d = tuple(jnp.tile(t, (bd, 1)) for t in _rope_tables(pos_d))
    lam_rows = jnp.stack([wts["lambda_q1"], wts["lambda_k1"], wts["lambda_q2"], wts["lambda_k2"]])
    qk = dict(d_q=d_q, d_k=d_k, d_v=d_v)

    blk = _tile(seq, ATTN_BLOCK, LANES)
    qt, kp_f32, k0, k1, vp_f32, vt = _qkv_post(proj_p, tab_p, wts["q_norm_w"], wts["k_norm_w"],
                                               seq_blocks=seq // blk, **qk)
    o_p = _attn_prompt(qt, k0, k1, vt, lam_rows, wts["subln_w"], lam_init=lam_init)
    act = _conv_prompt(proj_p, c_x, wts["conv_w"], wts["conv_b"], batch=bp, seq=seq)
    y_p, ssm_p = _ssd_prompt(act, dt_p, proj_p, c_z, wts["dt_bias"], wts["a_log"], wts["d_skip"],
                             wts["ssm_norm_w"], batch=bp, seq=seq, d_ssm=d_ssm)
    conv_p = proj_p.reshape(bp, seq, c_dt)[:, seq - (CONV_WIDTH - 1):, c_x:]

    q_d, kd_f32, vd_f32 = _qkv_post(proj_d, tab_d, wts["q_norm_w"], wts["k_norm_w"], **qk)
    o_d = _attn_decode(q_d, kd_f32, vd_f32, dec["cache_k"], dec["cache_v"], dec["page_table"],
                       lam_rows, wts["subln_w"], page=dec["page"], dseq=dseq, lam_init=lam_init)
    xbc_d = proj_d[:, c_x:]
    y_d, ssm_d = _ssd_decode(xbc_d, dec["state_conv"], dt_d, proj_d[:, c_z:c_x], 0, dec["state_ssm"],
                             wts["conv_w"], wts["conv_b"], wts["dt_bias"], wts["a_log"], wts["d_skip"],
                             wts["ssm_norm_w"], dseq=dseq, d_ssm=d_ssm)
    conv_d = jnp.concatenate([dec["state_conv"], xbc_d.reshape(bd, dseq, conv_dim)],
                             axis=1)[:, -(CONV_WIDTH - 1):]

    x1_p, x1_d = _matmul_w32([o_p, y_p], [o_d, y_d], wts["w_out"], n_out=d_model, res=xp, resd=xd,
                             name="out_proj")
    h2_p, h2_d = _rmsnorm(x1_p, wts["ln2_w"]), _rmsnorm(x1_d, wts["ln2_w"])
    up_p, up_d = _matmul_w32([h2_p], [h2_d], wts["w_up"], n_out=wts["w_up"].shape[1], act="relu2",
                             out_dtype=BF16, tm=1024, name="ffn_up")
    d_ff = wts["w_down"].shape[0]
    kc = _tile(d_ff, FFN_DOWN_CHUNK, LANES)
    out_p, out_d = x1_p, x1_d
    for c in range(d_ff // kc):
        out_p, out_d = _matmul_w32([up_p], [up_d], wts["w_down"], n_out=d_model, res=out_p, resd=out_d,
                                   k_chunk=(c, kc), name="ffn_down")

    def heads(a, b, s, w):
        return a.reshape(b, s, n_kv, w)

    state = lambda a, b: a.reshape(b, n_ssm_heads, SSM_HEAD_DIM, SSM_STATE)
    return (out_p, out_d,
            (heads(kp_f32, bp, seq, LANES), heads(vp_f32, bp, seq, ATTN_VHEAD), state(ssm_p, bp), conv_p,
             heads(kd_f32, bd, dseq, LANES), heads(vd_f32, bd, dseq, ATTN_VHEAD), state(ssm_d, bd), conv_d))


def kernel(x_prompt, x_sample, cache_k, cache_v, state_ssm, state_conv, page_table, ln1_w, w_in, q_norm_w, k_norm_w, lambda_q1, lambda_k1, lambda_q2, lambda_k2, subln_w, conv_w, conv_b, dt_bias, a_log, d_skip, ssm_norm_w, w_out, ln2_w, w_up, w_down):
    depth = w_in.shape[0]
    bp, seq, d_model = x_prompt.shape
    bs, dseq, _ = x_sample.shape
    page = cache_k.shape[2]
    past_len = page_table.shape[1] * page
    pos_p = jnp.arange(seq, dtype=jnp.int32)
    pos_s = past_len + jnp.arange(dseq, dtype=jnp.int32)

    yp = x_prompt.reshape(bp * seq, d_model)
    ys = x_sample.reshape(bs * dseq, d_model)
    outs = [[] for _ in range(8)]
    for l in range(depth):
        lam_init = 0.8 - 0.6 * math.exp(-0.3 * l)
        w_in_t = w_in[l].T
        wts = dict(
            ln1_w=ln1_w[l], w_in_t=w_in_t,
            q_norm_w=q_norm_w[l], k_norm_w=k_norm_w[l],
            lambda_q1=lambda_q1[l], lambda_k1=lambda_k1[l], lambda_q2=lambda_q2[l], lambda_k2=lambda_k2[l],
            subln_w=subln_w[l], conv_w=conv_w[l], conv_b=conv_b[l], dt_bias=dt_bias[l], a_log=a_log[l],
            d_skip=d_skip[l], ssm_norm_w=ssm_norm_w[l], w_out=w_out[l],
            ln2_w=ln2_w[l], w_up=w_up[l], w_down=w_down[l])
        dec = dict(cache_k=cache_k[l].reshape(-1, cache_k.shape[-1]), cache_v=cache_v[l].reshape(-1, cache_v.shape[-1]),
                   page=page, page_table=page_table, state_conv=state_conv[l], state_ssm=state_ssm[l])
        yp, ys, layer_outs = _layer(yp, ys, bp=bp, seq=seq, bd=bs, dseq=dseq, pos_p=pos_p, pos_d=pos_s,
                                    wts=wts, dec=dec, lam_init=lam_init)
        for lst, val in zip(outs, layer_outs):
            lst.append(val)
    stacked = [jnp.stack(lst) for lst in outs]
    return (yp.reshape(bp, seq, d_model), ys.reshape(bs, dseq, d_model), *stacked)
```

```python
import functools
import math

import jax
import jax.numpy as jnp
from jax import lax
from jax.experimental import pallas as pl
from jax.experimental.pallas import tpu as pltpu

F32 = jnp.float32
BF16 = jnp.bfloat16

EPS = 1e-6
LANES = 128
SUBLANES = 8
VMEM_LIMIT = 60 * 1024 * 1024

ATTN_VHEAD = 128
ATTN_SUB = 64
ROT_DIM = 16
ROPE_THETA = 500000.0
KV_GROUP = 4
SSM_HEAD_DIM = 64
SSM_STATE = 128
N_SSM_GROUPS = 8
CONV_WIDTH = 4
SSD_CHUNK = 128
FFN_DOWN_CHUNK = 4096
NEG_INF = float("-inf")


def _cparams(sem):
    return pltpu.CompilerParams(dimension_semantics=sem, vmem_limit_bytes=VMEM_LIMIT)


def _tile(n, target, mult=1, also=()):
    t = min(n, target)
    while t > 0:
        if n % t == 0 and t % mult == 0 and all(a % t == 0 for a in also):
            return t
        t -= 1
    raise ValueError((n, target, mult, also))


def _rmsnorm_kernel(x_ref, w_ref, o_ref):
    x = x_ref[...]
    ms = jnp.mean(x * x, axis=-1, keepdims=True)
    o_ref[...] = (x * lax.rsqrt(ms + EPS) * w_ref[...]).astype(o_ref.dtype)


def _rmsnorm(x, w, out_dtype=BF16):
    m, d = x.shape
    tr = _tile(m, 256, SUBLANES)
    return pl.pallas_call(
        _rmsnorm_kernel,
        out_shape=jax.ShapeDtypeStruct((m, d), out_dtype),
        grid=(m // tr,),
        in_specs=[pl.BlockSpec((tr, d), lambda i: (i, 0)),
                  pl.BlockSpec((1, d), lambda i: (0, 0))],
        out_specs=pl.BlockSpec((tr, d), lambda i: (i, 0)),
        compiler_params=_cparams(("parallel",)),
        name="rmsnorm",
    )(x, w.reshape(1, d))


def _mm_kernel(*refs, n_pairs, has_res, nk, act):
    pairs = refs[:2 * n_pairs]
    pos = 2 * n_pairs
    res_ref = refs[pos] if has_res else None
    o_ref = refs[pos + int(has_res)]

    def product():
        part = None
        for p in range(n_pairs):
            x = pairs[2 * p][...]
            if x.dtype != BF16:
                x = x.astype(BF16)
            d = jnp.dot(x, pairs[2 * p + 1][...], preferred_element_type=F32)
            part = d if part is None else part + d
        return part

    if nk == 1:
        part = product()
        if act == "relu2":
            part = jnp.square(jnp.maximum(part, 0.0))
        if has_res:
            part = res_ref[...] + part
        o_ref[...] = part.astype(o_ref.dtype)
    else:
        k = pl.program_id(2)

        @pl.when(k == 0)
        def _():
            o_ref[...] = (res_ref[...] + product()) if has_res else product()

        @pl.when(k > 0)
        def _():
            o_ref[...] += product()


def _matmul(pairs, *, n_out, w_col0=0, res=None, act=None, out_dtype=F32,
            tm=1024, tn=1024, tk=None, name="matmul"):
    m, kdim = pairs[0][0].shape
    tm = _tile(m, tm, SUBLANES)
    tn = _tile(n_out, tn, LANES, also=(w_col0,))
    tk = kdim if tk is None else _tile(kdim, tk, LANES)
    nk = kdim // tk
    if nk > 1:
        assert act is None and out_dtype == F32
    c0 = w_col0 // tn
    in_specs, args = [], []
    for x, w in pairs:
        assert x.shape == (m, kdim) and w.shape[0] == kdim
        in_specs += [pl.BlockSpec((tm, tk), lambda i, j, k: (i, k)),
                     pl.BlockSpec((tk, tn), lambda i, j, k: (k, j + c0))]
        args += [x, w]
    if res is not None:
        in_specs.append(pl.BlockSpec((tm, tn), lambda i, j, k: (i, j)))
        args.append(res)
    kern = functools.partial(_mm_kernel, n_pairs=len(pairs), has_res=res is not None, nk=nk, act=act)
    return pl.pallas_call(
        kern,
        out_shape=jax.ShapeDtypeStruct((m, n_out), out_dtype),
        grid=(m // tm, n_out // tn, nk),
        in_specs=in_specs,
        out_specs=pl.BlockSpec((tm, tn), lambda i, j, k: (i, j)),
        compiler_params=_cparams(("parallel", "parallel", "arbitrary")),
        name=name,
    )(*args)


def _mmw_kernel(*refs, n_x, has_res, act, tn, c0, nj, w_is_transposed, w_row0, kdim):
    xs = refs[:n_x]
    w_any = refs[n_x]
    xds = refs[n_x + 1:2 * n_x + 1]
    pos = 2 * n_x + 1
    res_ref = resd_ref = None
    if has_res:
        res_ref, resd_ref = refs[pos], refs[pos + 1]
        pos += 2
    o_ref, od_ref, wf_scr, wb_scr, sem = refs[pos:pos + 5]
    j, i = pl.program_id(0), pl.program_id(1)

    def w_copy(jj):
        col = pl.multiple_of((jj + c0) * tn, tn)
        rows = pl.ds(w_row0, kdim)
        panel = w_any.at[pl.ds(col, tn), rows] if w_is_transposed else w_any.at[rows, pl.ds(col, tn)]
        return pltpu.make_async_copy(panel, wf_scr, sem)

    def product(x_refs):
        part, k0 = None, 0
        for xr in x_refs:
            x = xr[...]
            if x.dtype != BF16:
                x = x.astype(BF16)
            kp = x.shape[1]
            d = jnp.dot(x, wb_scr[k0:k0 + kp, :], preferred_element_type=F32)
            part = d if part is None else part + d
            k0 += kp
        return part

    def finish(part, r_ref, out_ref):
        if act == "relu2":
            part = jnp.square(jnp.maximum(part, 0.0))
        if r_ref is not None:
            part = r_ref[...] + part
        out_ref[...] = part.astype(out_ref.dtype)

    @pl.when(i == 0)
    def _():
        @pl.when(j == 0)
        def _():
            w_copy(0).start()

        w_copy(j).wait()
        panel = wf_scr[...]
        wb_scr[...] = (panel.T if w_is_transposed else panel).astype(BF16)

        @pl.when(j + 1 < nj)
        def _():
            w_copy(j + 1).start()

        finish(product(xds), resd_ref, od_ref)

    finish(product(xs), res_ref, o_ref)


def _matmul_w32(xs, xds, w, *, n_out, w_col0=0, res=None, resd=None, act=None, out_dtype=F32,
                tm=512, tn=1024, w_is_transposed=False, k_chunk=None, name="matmul_w32"):
    m, md = xs[0].shape[0], xds[0].shape[0]
    if k_chunk is None:
        kdim = w.shape[1] if w_is_transposed else w.shape[0]
        widths, xcb, w_row0 = [x.shape[1] for x in xs], 0, 0
        assert sum(widths) == kdim and widths == [x.shape[1] for x in xds]
    else:
        xcb, kdim = k_chunk
        widths, w_row0 = [kdim], xcb * kdim
        assert len(xs) == 1 and xs[0].shape[1] % kdim == 0
    tm = _tile(m, tm, SUBLANES)
    tn = _tile(n_out, tn, LANES, also=(w_col0,))
    nj = n_out // tn
    in_specs = [pl.BlockSpec((tm, kp), lambda j, i: (i, xcb)) for kp in widths]
    in_specs.append(pl.BlockSpec(memory_space=pl.ANY))
    in_specs += [pl.BlockSpec((md, kp), lambda j, i: (0, xcb)) for kp in widths]
    args = [*xs, w, *xds]
    if res is not None:
        in_specs += [pl.BlockSpec((tm, tn), lambda j, i: (i, j)), pl.BlockSpec((md, tn), lambda j, i: (0, j))]
        args += [res, resd]
    kern = functools.partial(_mmw_kernel, n_x=len(xs), has_res=res is not None, act=act, tn=tn,
                             c0=w_col0 // tn, nj=nj, w_is_transposed=w_is_transposed, w_row0=w_row0,
                             kdim=kdim)
    wf_shape = (tn, kdim) if w_is_transposed else (kdim, tn)
    return pl.pallas_call(
        kern,
        out_shape=(jax.ShapeDtypeStruct((m, n_out), out_dtype), jax.ShapeDtypeStruct((md, n_out), out_dtype)),
        grid=(nj, m // tm),
        in_specs=in_specs,
        out_specs=(pl.BlockSpec((tm, tn), lambda j, i: (i, j)), pl.BlockSpec((md, tn), lambda j, i: (0, j))),
        scratch_shapes=[pltpu.VMEM(wf_shape, F32), pltpu.VMEM((kdim, tn), BF16), pltpu.SemaphoreType.DMA],
        compiler_params=_cparams(("arbitrary", "arbitrary")),
        name=name,
    )(*args)


def _dt_proj_kernel(h_ref, w_ref, o_ref):
    n_dt = w_ref.shape[0]
    dt = _dot_nt(h_ref[...], w_ref[...].astype(BF16))
    o_ref[...] = jnp.concatenate([dt, jnp.zeros((dt.shape[0], LANES - n_dt), F32)], axis=1)


def _dt_proj(h, w_t, row0, n_dt):
    m, kdim = h.shape
    tm = _tile(m, 1024, SUBLANES)
    assert row0 % n_dt == 0 and n_dt % SUBLANES == 0
    return pl.pallas_call(
        _dt_proj_kernel,
        out_shape=jax.ShapeDtypeStruct((m, LANES), F32),
        grid=(m // tm,),
        in_specs=[pl.BlockSpec((tm, kdim), lambda i: (i, 0)),
                  pl.BlockSpec((n_dt, kdim), lambda i: (row0 // n_dt, 0))],
        out_specs=pl.BlockSpec((tm, LANES), lambda i: (i, 0)),
        compiler_params=_cparams(("parallel",)),
        name="in_proj_dt",
    )(h, w_t)


def _norm_rope(x, w, cos, sa, sb):
    x2 = x * x
    x2_hi = x2.astype(BF16)
    x2_lo = (x2 - x2_hi.astype(F32)).astype(BF16)
    same = (lax.broadcasted_iota(jnp.int32, (LANES, LANES), 0) // ATTN_SUB
            == lax.broadcasted_iota(jnp.int32, (LANES, LANES), 1) // ATTN_SUB)
    seg = jnp.where(same, 1.0, 0.0).astype(BF16)
    ssq = (jnp.dot(x2_hi, seg, preferred_element_type=F32) + jnp.dot(x2_lo, seg, preferred_element_type=F32))
    ms = ssq * (1.0 / ATTN_SUB)
    y = x * lax.rsqrt(ms + EPS) * w
    up = pltpu.roll(y, LANES - ROT_DIM // 2, 1)
    dn = pltpu.roll(y, ROT_DIM // 2, 1)
    return y * cos + up * sa + dn * sb


def _qkv_post_decode_kernel(qkv_ref, cos_ref, sa_ref, sb_ref, qw_ref, kw_ref, q_ref, kf_ref, vf_ref,
                            *, d_q, d_k):
    cos, sa, sb = cos_ref[...], sa_ref[...], sb_ref[...]
    qw, kw = qw_ref[...], kw_ref[...]
    scale = ATTN_SUB ** -0.5
    for g in range(d_q // LANES):
        sl = slice(g * LANES, (g + 1) * LANES)
        q_ref[:, sl] = _norm_rope(qkv_ref[:, sl], qw, cos, sa, sb) * scale
    for g in range(d_k // LANES):
        sl = slice(g * LANES, (g + 1) * LANES)
        kf_ref[:, sl] = _norm_rope(qkv_ref[:, d_q + g * LANES:d_q + (g + 1) * LANES], kw, cos, sa, sb)
    vf_ref[...] = qkv_ref[:, d_q + d_k:]


def _qkv_post_prompt_kernel(qkv_ref, cos_ref, sa_ref, sb_ref, qw_ref, kw_ref,
                            qt_ref, kf_ref, k0_ref, k1_ref, vf_ref, vt_ref, *, d_q, d_k):
    tr = qkv_ref.shape[0]
    cos, sa, sb = cos_ref[...], sa_ref[...], sb_ref[...]
    lo = lax.broadcasted_iota(jnp.int32, (tr, LANES), 1) < ATTN_SUB
    qw, kw = qw_ref[...], kw_ref[...]
    scale = (ATTN_SUB ** -0.5) * math.log2(math.e)
    for hd in range(d_q // LANES):
        y = _norm_rope(qkv_ref[:, hd * LANES:(hd + 1) * LANES], qw, cos, sa, sb) * scale
        qt_ref[hd] = y.T.astype(BF16)
    n_kv = d_k // LANES
    for g in range(n_kv):
        y = _norm_rope(qkv_ref[:, d_q + g * LANES:d_q + (g + 1) * LANES], kw, cos, sa, sb)
        sl = slice(g * LANES, (g + 1) * LANES)
        kf_ref[pl.ds(g, tr, stride=n_kv), :] = y
        k0_ref[:, sl] = jnp.where(lo, y, 0.0).astype(BF16)
        k1_ref[:, sl] = jnp.where(lo, 0.0, y).astype(BF16)
    v = qkv_ref[:, d_q + d_k:]
    for g in range(v.shape[1] // LANES):
        vg = v[:, g * LANES:(g + 1) * LANES]
        vf_ref[pl.ds(g, tr, stride=n_kv), :] = vg
        vt_ref[g] = vg.T.astype(BF16)


def _rope_tables(pos):
    half = ROT_DIM // 2
    inv = jnp.exp(-math.log(ROPE_THETA) * jnp.arange(half, dtype=F32) * 2.0 / ROT_DIM)
    ang = pos.astype(F32)[:, None] * inv[None, :]
    cos, sin = jnp.cos(ang), jnp.sin(ang)
    n = pos.shape[0]
    ones = jnp.ones((n, ATTN_SUB - ROT_DIM), F32)
    zeros8 = jnp.zeros((n, half), F32)
    zeros = jnp.zeros((n, ATTN_SUB - ROT_DIM), F32)
    c = jnp.concatenate([cos, cos, ones], axis=1)
    sa = jnp.concatenate([-sin, zeros8, zeros], axis=1)
    sb = jnp.concatenate([zeros8, sin, zeros], axis=1)
    rep = LANES // ATTN_SUB
    return jnp.tile(c, (1, rep)), jnp.tile(sa, (1, rep)), jnp.tile(sb, (1, rep))


def _qkv_post(qkv, tables, q_norm_w, k_norm_w, *, d_q, d_k, d_v, seq_blocks=None):
    m = qkv.shape[0]
    n_tab = tables[0].shape[0]
    rep = LANES // ATTN_SUB
    row = lambda i: (i, 0)
    fix = lambda i: (0, 0)
    if seq_blocks is None:
        tr = _tile(m, 256, SUBLANES, also=(n_tab,))
    else:
        tr = n_tab // seq_blocks
    ntb = n_tab // tr
    tab = lambda i: (i % ntb, 0)
    in_specs = [pl.BlockSpec((tr, d_q + d_k + d_v), row),
                pl.BlockSpec((tr, LANES), tab), pl.BlockSpec((tr, LANES), tab),
                pl.BlockSpec((tr, LANES), tab),
                pl.BlockSpec((1, LANES), fix), pl.BlockSpec((1, LANES), fix)]
    args = (qkv, *tables, jnp.tile(q_norm_w, rep).reshape(1, LANES), jnp.tile(k_norm_w, rep).reshape(1, LANES))
    if seq_blocks is None:
        return pl.pallas_call(
            functools.partial(_qkv_post_decode_kernel, d_q=d_q, d_k=d_k),
            out_shape=(jax.ShapeDtypeStruct((m, d_q), F32), jax.ShapeDtypeStruct((m, d_k), F32),
                       jax.ShapeDtypeStruct((m, d_v), F32)),
            grid=(m // tr,),
            in_specs=in_specs,
            out_specs=(pl.BlockSpec((tr, d_q), row), pl.BlockSpec((tr, d_k), row),
                       pl.BlockSpec((tr, d_v), row)),
            compiler_params=_cparams(("parallel",)),
            name="qkv_post_decode",
        )(*args)
    batch = m // n_tab
    n_heads, n_kv = d_q // LANES, d_v // LANES
    tmap = lambda i: (i // seq_blocks, 0, i % seq_blocks, 0, 0)
    return pl.pallas_call(
        functools.partial(_qkv_post_prompt_kernel, d_q=d_q, d_k=d_k),
        out_shape=(jax.ShapeDtypeStruct((batch, n_heads, seq_blocks, LANES, tr), BF16),
                   jax.ShapeDtypeStruct((m * n_kv, LANES), F32),
                   jax.ShapeDtypeStruct((m, d_k), BF16), jax.ShapeDtypeStruct((m, d_k), BF16),
                   jax.ShapeDtypeStruct((m * n_kv, LANES), F32),
                   jax.ShapeDtypeStruct((batch, n_kv, seq_blocks, LANES, tr), BF16)),
        grid=(m // tr,),
        in_specs=in_specs,
        out_specs=(pl.BlockSpec((None, n_heads, None, LANES, tr), tmap),
                   pl.BlockSpec((tr * n_kv, LANES), row), pl.BlockSpec((tr, d_k), row),
                   pl.BlockSpec((tr, d_k), row), pl.BlockSpec((tr * n_kv, LANES), row),
                   pl.BlockSpec((None, n_kv, None, LANES, tr), tmap)),
        compiler_params=_cparams(("parallel",)),
        name="qkv_post_prompt",
    )(*args)


def _lambda_value(lam_ref, lam_init):
    l = lam_ref[...]
    d1 = jnp.sum(l[0:1] * l[1:2], axis=-1, keepdims=True)
    d2 = jnp.sum(l[2:3] * l[3:4], axis=-1, keepdims=True)
    return jnp.exp(d1) - jnp.exp(d2) + lam_init


def _subln(o, w, lam_init):
    ms = jnp.mean(o * o, axis=-1, keepdims=True)
    return (o * lax.rsqrt(ms + EPS) * w) * (1.0 - lam_init)


def _dot_nt(a, b):
    return lax.dot_general(a, b, (((1,), (1,)), ((), ())), preferred_element_type=F32)


ONES_ROWS = 16
ATTN_BLOCK = 256
ATTN_SPAN = 3


def _attn_prompt_kernel(qt_ref, k0_ref, k1_ref, vt_ref, lam_ref, subw_ref, o_ref, acc_scr,
                        *, blk, lam_init):
    i = pl.program_id(2)
    acc_scr[...] = jnp.zeros(acc_scr.shape, F32)
    k_refs = (k0_ref, k1_ref)

    def kv_span(j, n_blk, m_all, diagonal):
        nkeys = n_blk * blk
        v_blocks = jnp.concatenate([vt_ref[j + t] for t in range(n_blk)], axis=1)
        vt = jnp.concatenate([v_blocks, jnp.ones((ONES_ROWS, nkeys), BF16)], axis=0)
        start = pl.multiple_of(j * blk, blk)
        if diagonal:
            visible = (lax.broadcasted_iota(jnp.int32, (nkeys, blk), 0) - (nkeys - blk)
                       <= lax.broadcasted_iota(jnp.int32, (nkeys, blk), 1))
        kcs = [k_refs[c][pl.ds(start, nkeys), :] for c in range(2)]
        hcs = [(h, c) for h in range(KV_GROUP) for c in range(2)]
        scores = [jnp.dot(kcs[c], qt_ref[h], preferred_element_type=F32) for h, c in hcs]
        m_rows, alphas, probs = [], [], []
        for idx, s in enumerate(scores):
            if diagonal:
                s = jnp.where(visible, s, NEG_INF)
            m_old = m_all[idx:idx + 1, :]
            m_new = jnp.maximum(m_old, jnp.max(s, axis=0, keepdims=True))
            probs.append(jnp.exp2(s - m_new).astype(BF16))
            alphas.append(jnp.exp2(m_old - m_new))
            m_rows.append(m_new)
        for idx, p in enumerate(probs):
            pv = jnp.dot(vt, p, preferred_element_type=F32)
            acc_scr[idx] = alphas[idx] * acc_scr[idx] + pv
        return jnp.concatenate(m_rows, axis=0)

    m_init = jnp.full((2 * KV_GROUP, blk), NEG_INF, F32)
    m_all = lax.fori_loop(0, i // ATTN_SPAN,
                          lambda js, m: kv_span(ATTN_SPAN * js, ATTN_SPAN, m, False), m_init)
    for tail in range(ATTN_SPAN):
        @pl.when(i % ATTN_SPAN == tail)
        def _(tail=tail):
            kv_span(i - tail, tail + 1, m_all, True)

    lam = _lambda_value(lam_ref, lam_init)
    w = subw_ref[...]
    for h in range(KV_GROUP):
        a0, a1 = acc_scr[2 * h], acc_scr[2 * h + 1]
        o0 = a0[:LANES] / a0[LANES:LANES + 1]
        o1 = a1[:LANES] / a1[LANES:LANES + 1]
        o = _subln((o0 - lam * o1).T, w, lam_init)
        o_ref[:, h * LANES:(h + 1) * LANES] = o.astype(o_ref.dtype)


def _attn_prompt(qt, k0, k1, vt, lam_rows, subln_w, *, lam_init):
    batch, n_heads, nb, _, blk = qt.shape
    n_kv = vt.shape[1]
    seq = nb * blk
    gw = KV_GROUP * LANES
    fix = lambda b, g, i: (0, 0)
    kern = functools.partial(_attn_prompt_kernel, blk=blk, lam_init=lam_init)
    return pl.pallas_call(
        kern,
        out_shape=jax.ShapeDtypeStruct((batch * seq, n_heads * LANES), BF16),
        grid=(batch, n_kv, nb),
        in_specs=[pl.BlockSpec((None, KV_GROUP, None, LANES, blk), lambda b, g, i: (b, g, i, 0, 0)),
                  pl.BlockSpec((seq, LANES), lambda b, g, i: (b, g)),
                  pl.BlockSpec((seq, LANES), lambda b, g, i: (b, g)),
                  pl.BlockSpec((None, None, nb, LANES, blk), lambda b, g, i: (b, g, 0, 0, 0)),
                  pl.BlockSpec((4, ATTN_SUB), fix),
                  pl.BlockSpec((1, LANES), fix)],
        out_specs=pl.BlockSpec((blk, gw), lambda b, g, i: (b * nb + i, g)),
        scratch_shapes=[pltpu.VMEM((2 * KV_GROUP, LANES + ONES_ROWS, blk), F32)],
        compiler_params=_cparams(("parallel", "parallel", "parallel")),
        name="attn_prompt",
    )(qt, k0, k1, vt, lam_rows, subln_w.reshape(1, LANES))


DECODE_SLOTS = 3


def _attn_decode_kernel(pt_ref, q_ref, kn_ref, vn_ref, lam_ref, subw_ref, ck_any, cv_any, o_ref,
                        qf_scr, qb_scr, kbuf, vbuf, kc_scr, vc_scr, m_scr, l_scr, acc_scr, sem,
                        *, pps, n_steps, n_b, page, dseq, n_kv, lam_init):
    b, j = pl.program_id(0), pl.program_id(1)
    rows = qf_scr.shape[0]
    blk = 2 * dseq
    prow = page * n_kv
    n_chunks = n_b * n_steps

    def chunk_copies(c):
        cb, cj, slot = c // n_steps, c % n_steps, c % DECODE_SLOTS
        copies = []
        for t in range(pps):
            src = pl.ds(pl.multiple_of(pt_ref[cb, cj * pps + t] * prow, prow), prow)
            dst = pl.ds(t * prow, prow)
            copies.append(pltpu.make_async_copy(ck_any.at[src, :], kbuf.at[slot, dst, :], sem.at[0, slot]))
            copies.append(pltpu.make_async_copy(cv_any.at[src, :], vbuf.at[slot, dst, :], sem.at[1, slot]))
        return copies

    @pl.when(j == 0)
    def _():
        qf_scr[...] = jnp.zeros(qf_scr.shape, F32)
        lo = lax.broadcasted_iota(jnp.int32, (dseq, LANES), 1) < ATTN_SUB
        for g in range(n_kv):
            for h in range(KV_GROUP):
                hd = g * KV_GROUP + h
                qh = q_ref[:, hd * LANES:(hd + 1) * LANES]
                r0 = hd * blk
                qf_scr[r0:r0 + dseq, g * LANES:(g + 1) * LANES] = jnp.where(lo, qh, 0.0)
                qf_scr[r0 + dseq:r0 + blk, g * LANES:(g + 1) * LANES] = jnp.where(lo, 0.0, qh)
        qb_scr[...] = qf_scr[...].astype(BF16)
        m_scr[...] = jnp.full(m_scr.shape, NEG_INF, F32)
        l_scr[...] = jnp.zeros(l_scr.shape, F32)
        acc_scr[...] = jnp.zeros(acc_scr.shape, F32)

    n_grp = max(1, n_kv // 2)
    gr, gc = rows // n_grp, kc_scr.shape[1] // n_grp

    def online_update(k, v, mask=None):
        s = jnp.concatenate([_dot_nt(qb_scr[t * gr:(t + 1) * gr, t * gc:(t + 1) * gc],
                                     k[:, t * gc:(t + 1) * gc]) for t in range(n_grp)], axis=0)
        if mask is not None:
            s = jnp.where(mask, s, NEG_INF)
        m_old = m_scr[...]
        m_new = jnp.maximum(m_old, jnp.max(s, axis=-1, keepdims=True))
        p = jnp.exp(s - jnp.tile(m_new, (1, s.shape[1] // LANES)))
        alpha = jnp.exp(m_old - m_new)
        l_scr[...] = alpha * l_scr[...] + jnp.sum(p, axis=-1, keepdims=True)
        p16 = p.astype(BF16)
        for t in range(n_grp):
            rs, cs = slice(t * gr, (t + 1) * gr), slice(t * gc, (t + 1) * gc)
            pv = jnp.dot(p16[rs], v[:, cs], preferred_element_type=F32)
            acc_scr[rs, cs] = jnp.tile(alpha[rs], (1, gc // LANES)) * acc_scr[rs, cs] + pv
        m_scr[...] = m_new

    @pl.when(j < n_steps)
    def _():
        c = b * n_steps + j

        @pl.when(c == 0)
        def _():
            for ahead in range(min(DECODE_SLOTS - 1, n_chunks)):
                for cp in chunk_copies(ahead):
                    cp.start()

        @pl.when(c + DECODE_SLOTS - 1 < n_chunks)
        def _():
            for cp in chunk_copies(c + DECODE_SLOTS - 1):
                cp.start()

        for cp in chunk_copies(c):
            cp.wait()
        slot = c % DECODE_SLOTS
        for t in range(pps):
            for g in range(n_kv):
                head_rows = pl.ds(t * prow + g, page, stride=n_kv)
                cols = slice(g * LANES, (g + 1) * LANES)
                kc_scr[t * page:(t + 1) * page, cols] = kbuf[slot, head_rows, :].astype(BF16)
                vc_scr[t * page:(t + 1) * page, cols] = vbuf[slot, head_rows, :].astype(BF16)
        online_update(kc_scr, vc_scr)

    @pl.when(j == n_steps)
    def _():
        kpad = jnp.concatenate([kn_ref[...], jnp.zeros((page - dseq, kn_ref.shape[1]), F32)], axis=0)
        vpad = jnp.concatenate([vn_ref[...], jnp.zeros((page - dseq, vn_ref.shape[1]), F32)], axis=0)
        t_idx = lax.broadcasted_iota(jnp.int32, (rows, page), 0) % dseq
        u_idx = lax.broadcasted_iota(jnp.int32, (rows, page), 1)
        online_update(kpad.astype(BF16), vpad.astype(BF16), mask=u_idx <= t_idx)

        lam = _lambda_value(lam_ref, lam_init)
        w = subw_ref[...]
        for g in range(n_kv):
            for h in range(KV_GROUP):
                hd = g * KV_GROUP + h
                r0 = hd * blk
                cs = slice(g * LANES, (g + 1) * LANES)
                o0 = acc_scr[r0:r0 + dseq, cs] / l_scr[r0:r0 + dseq, :]
                o1 = acc_scr[r0 + dseq:r0 + blk, cs] / l_scr[r0 + dseq:r0 + blk, :]
                o_ref[:, hd * LANES:(hd + 1) * LANES] = _subln(o0 - lam * o1, w, lam_init)


def _attn_decode(q, k_new, v_new, cache_k, cache_v, page_table, lam_rows, subln_w, *, page, dseq, lam_init):
    m, d_q = q.shape
    n_b, n_pages = page_table.shape
    kvw = k_new.shape[1]
    n_kv = kvw // LANES
    pps = _tile(n_pages, 8)
    n_steps = n_pages // pps
    rows = n_kv * KV_GROUP * 2 * dseq

    tok = lambda b, j, pt: (b, 0)
    fix = lambda b, j, pt: (0, 0)
    chunk_rows = pps * page * n_kv
    kern = functools.partial(_attn_decode_kernel, pps=pps, n_steps=n_steps, n_b=n_b, page=page, dseq=dseq,
                             n_kv=n_kv, lam_init=lam_init)
    return pl.pallas_call(
        kern,
        out_shape=jax.ShapeDtypeStruct((m, d_q), F32),
        grid_spec=pltpu.PrefetchScalarGridSpec(
            num_scalar_prefetch=1,
            grid=(n_b, n_steps + 1),
            in_specs=[pl.BlockSpec((dseq, d_q), tok),
                      pl.BlockSpec((dseq, kvw), tok), pl.BlockSpec((dseq, kvw), tok),
                      pl.BlockSpec((4, ATTN_SUB), fix), pl.BlockSpec((1, LANES), fix),
                      pl.BlockSpec(memory_space=pl.ANY), pl.BlockSpec(memory_space=pl.ANY)],
            out_specs=pl.BlockSpec((dseq, d_q), tok),
            scratch_shapes=[pltpu.VMEM((rows, kvw), F32), pltpu.VMEM((rows, kvw), BF16),
                            pltpu.VMEM((DECODE_SLOTS, chunk_rows, LANES), F32),
                            pltpu.VMEM((DECODE_SLOTS, chunk_rows, LANES), F32),
                            pltpu.VMEM((pps * page, kvw), BF16), pltpu.VMEM((pps * page, kvw), BF16),
                            pltpu.VMEM((rows, LANES), F32), pltpu.VMEM((rows, LANES), F32),
                            pltpu.VMEM((rows, kvw), F32),
                            pltpu.SemaphoreType.DMA((2, DECODE_SLOTS))]),
        compiler_params=_cparams(("arbitrary", "arbitrary")),
        name="attn_decode",
    )(page_table, q, k_new, v_new, lam_rows, subln_w.reshape(1, LANES), cache_k, cache_v)


def _silu(x):
    return x * (1.0 / (1.0 + jnp.exp(-x)))


def _conv_taps(xp, w, bias, n_rows):
    acc = None
    for t in range(CONV_WIDTH):
        shift = CONV_WIDTH - 1 - t
        xs = xp if shift == 0 else pltpu.roll(xp, shift, 0)
        term = xs[SUBLANES:] * w[t:t + 1]
        acc = term if acc is None else acc + term
    return _silu(acc + bias)


def _conv_prompt_kernel(x_ref, halo_ref, w_ref, b_ref, o_ref):
    i = pl.program_id(1)
    halo = jnp.where(i > 0, halo_ref[...], 0.0)
    xp = jnp.concatenate([halo, x_ref[...]], axis=0)
    o_ref[...] = _conv_taps(xp, w_ref[...], b_ref[...], x_ref.shape[0])


def _conv_prompt(src, col0, conv_w, conv_b, *, batch, seq):
    m = src.shape[0]
    c = conv_w.shape[1]
    tr = _tile(seq, 256, SUBLANES)
    tc = _tile(c, 1024, LANES, also=(col0,))
    nr = seq // tr
    hb = tr // SUBLANES
    cb0 = col0 // tc
    return pl.pallas_call(
        _conv_prompt_kernel,
        out_shape=jax.ShapeDtypeStruct((m, c), F32),
        grid=(batch, nr, c // tc),
        in_specs=[pl.BlockSpec((tr, tc), lambda b, i, j: (b * nr + i, j + cb0)),
                  pl.BlockSpec((SUBLANES, tc),
                               lambda b, i, j: (jnp.maximum((b * nr + i) * hb - 1, 0), j + cb0)),
                  pl.BlockSpec((CONV_WIDTH, tc), lambda b, i, j: (0, j)),
                  pl.BlockSpec((1, tc), lambda b, i, j: (0, j))],
        out_specs=pl.BlockSpec((tr, tc), lambda b, i, j: (b * nr + i, j)),
        compiler_params=_cparams(("parallel", "parallel", "parallel")),
        name="conv_prompt",
    )(src, src, conv_w, conv_b.reshape(1, c))


def _cumsum_rows(x):
    n = x.shape[0]
    row = lax.broadcasted_iota(jnp.int32, x.shape, 0)
    s = 1
    while s < n:
        x = x + jnp.where(row >= s, pltpu.roll(x, s, 0), 0.0)
        s *= 2
    return x


def _ssd_chunk(xs, bm, cm, dt_raw, z, state_ref, dtb, alog, dskip, normw, *, n_valid):
    cl, d_ssm = xs.shape
    n_heads = d_ssm // SSM_HEAD_DIM
    hpg = n_heads // N_SSM_GROUPS
    gs = d_ssm // N_SSM_GROUPS
    x_dt = dt_raw + dtb
    dtp = jnp.maximum(x_dt, 0.0) + jnp.log1p(jnp.exp(-jnp.abs(x_dt)))
    if n_valid < cl:
        dtp = jnp.where(lax.broadcasted_iota(jnp.int32, dtp.shape, 0) < n_valid, dtp, 0.0)
    a = -jnp.exp(alog)
    cs = _cumsum_rows(dtp * a)
    cs_t = cs.T
    cs_last = cs[cl - 1:cl, :]
    dec_all = jnp.exp(cs_last)
    tri = lax.broadcasted_iota(jnp.int32, (cl, cl), 0) >= lax.broadcasted_iota(jnp.int32, (cl, cl), 1)
    lo = lax.broadcasted_iota(jnp.int32, (cl, LANES), 1) < SSM_HEAD_DIM
    lo_row = lo[0:1]
    hsel = lax.broadcasted_iota(jnp.int32, (2 * SSM_HEAD_DIM, SSM_STATE), 0) < SSM_HEAD_DIM
    n_pairs = n_heads // 2
    group_of = lambda pr: (2 * pr) // hpg

    b16 = [bm[:, g * SSM_STATE:(g + 1) * SSM_STATE].astype(BF16) for g in range(N_SSM_GROUPS)]
    c16 = [cm[:, g * SSM_STATE:(g + 1) * SSM_STATE].astype(BF16) for g in range(N_SSM_GROUPS)]
    cb = [_dot_nt(c16[g], b16[g]) for g in range(N_SSM_GROUPS)]
    states = [state_ref[pr * LANES:(pr + 1) * LANES, :] for pr in range(n_pairs)]
    y_off_raw = [_dot_nt(c16[group_of(pr)], states[pr].astype(BF16)) for pr in range(n_pairs)]

    x_pairs, xdt16, xdec16, scores, dec_in = [], [], [], [], []
    for pr in range(n_pairs):
        h0 = 2 * pr
        full = [jnp.broadcast_to(cs[:, h:h + 1], (cl, LANES)) for h in (h0, h0 + 1)]
        cs_pair = jnp.where(lo, full[0], full[1])
        dt_pair = jnp.where(lo, dtp[:, h0:h0 + 1], dtp[:, h0 + 1:h0 + 2])
        last_pair = jnp.where(lo_row, cs_last[:, h0:h0 + 1], cs_last[:, h0 + 1:h0 + 2])
        x_pair = xs[:, pr * LANES:(pr + 1) * LANES]
        xdt = x_pair * dt_pair
        x_pairs.append(x_pair)
        xdt16.append(xdt.astype(BF16))
        xdec16.append((xdt * jnp.exp(last_pair - cs_pair)).astype(BF16))
        dec_in.append(jnp.exp(cs_pair))
        for e in range(2):
            seg = full[e] - cs_t[h0 + e:h0 + e + 1, :]
            lmat = jnp.exp(jnp.where(tri, seg, NEG_INF))
            scores.append((cb[group_of(pr)] * lmat).astype(BF16))

    y_parts = []
    for pr in range(n_pairs):
        h0 = 2 * pr
        halves = [jnp.dot(scores[h0 + e], xdt16[pr], preferred_element_type=F32) for e in range(2)]
        y_diag = jnp.where(lo, halves[0], halves[1])
        skip = jnp.where(lo_row, dskip[:, h0:h0 + 1], dskip[:, h0 + 1:h0 + 2])
        y_parts.append(y_diag + y_off_raw[pr] * dec_in[pr] + skip * x_pairs[pr])
        new = lax.dot_general(xdec16[pr], b16[group_of(pr)], (((0,), (0,)), ((), ())),
                              preferred_element_type=F32)
        dec = jnp.where(hsel, dec_all[:, h0:h0 + 1], dec_all[:, h0 + 1:h0 + 2])
        state_ref[pr * LANES:(pr + 1) * LANES, :] = dec * states[pr] + new
    y = jnp.concatenate(y_parts, axis=1) * _silu(z)
    outs = []
    for g in range(N_SSM_GROUPS):
        yg = y[:, g * gs:(g + 1) * gs]
        ms = jnp.mean(yg * yg, axis=-1, keepdims=True)
        outs.append(yg * lax.rsqrt(ms + EPS) * normw[:, g * gs:(g + 1) * gs])
    return jnp.concatenate(outs, axis=1)


def _ssd_prompt_kernel(xs_ref, b_ref, c_ref, dt_ref, *rest, n_z):
    z_refs = rest[:n_z]
    dtb_ref, alog_ref, dskip_ref, normw_ref, y_ref, st_ref = rest[n_z:]

    @pl.when(pl.program_id(1) == 0)
    def _():
        st_ref[...] = jnp.zeros(st_ref.shape, F32)

    z = jnp.concatenate([r[...] for r in z_refs], axis=1)
    y = _ssd_chunk(xs_ref[...], b_ref[...], c_ref[...], dt_ref[...], z, st_ref,
                   dtb_ref[...], alog_ref[...], dskip_ref[...], normw_ref[...], n_valid=xs_ref.shape[0])
    y_ref[...] = y.astype(y_ref.dtype)


def _pad_lanes(v):
    return jnp.pad(v, (0, LANES - v.shape[0])).reshape(1, LANES)


def _ssd_prompt(xbc_act, dt, z_src, z_col0, dt_bias, a_log, d_skip, ssm_norm_w, *, batch, seq, d_ssm):
    m = xbc_act.shape[0]
    cl = min(SSD_CHUNK, seq)
    nc = seq // cl
    gn = N_SSM_GROUPS * SSM_STATE
    n_heads = d_ssm // SSM_HEAD_DIM
    row = lambda b, c: (b * nc + c, 0)
    fix = lambda b, c: (0, 0)
    assert d_ssm % gn == 0
    zw = _tile(d_ssm, d_ssm, LANES, also=(z_col0,))
    n_z = d_ssm // zw

    def z_map(t):
        return lambda b, c: (b * nc + c, z_col0 // zw + t)

    return pl.pallas_call(
        functools.partial(_ssd_prompt_kernel, n_z=n_z),
        out_shape=(jax.ShapeDtypeStruct((m, d_ssm), BF16),
                   jax.ShapeDtypeStruct((batch, n_heads * SSM_HEAD_DIM, SSM_STATE), F32)),
        grid=(batch, nc),
        in_specs=[pl.BlockSpec((cl, d_ssm), row),
                  pl.BlockSpec((cl, gn), lambda b, c: (b * nc + c, d_ssm // gn)),
                  pl.BlockSpec((cl, gn), lambda b, c: (b * nc + c, d_ssm // gn + 1)),
                  pl.BlockSpec((cl, LANES), row)]
                 + [pl.BlockSpec((cl, zw), z_map(t)) for t in range(n_z)]
                 + [pl.BlockSpec((1, LANES), fix), pl.BlockSpec((1, LANES), fix),
                    pl.BlockSpec((1, LANES), fix), pl.BlockSpec((1, d_ssm), fix)],
        out_specs=(pl.BlockSpec((cl, d_ssm), row),
                   pl.BlockSpec((None, n_heads * SSM_HEAD_DIM, SSM_STATE), lambda b, c: (b, 0, 0))),
        compiler_params=_cparams(("parallel", "arbitrary")),
        name="ssd_prompt",
    )(xbc_act, xbc_act, xbc_act, dt, *([z_src] * n_z), _pad_lanes(dt_bias), _pad_lanes(a_log),
      _pad_lanes(d_skip), ssm_norm_w.reshape(1, d_ssm))


def _ssd_decode_kernel(xbc_ref, prev_ref, dt_ref, z_ref, st0_ref, cw_ref, cb_ref, dtb_ref, alog_ref,
                       dskip_ref, normw_ref, y_ref, st_ref, *, dseq, d_ssm, cl):
    gn = N_SSM_GROUPS * SSM_STATE
    prev = prev_ref[...]
    c = prev.shape[1]
    xp = jnp.concatenate([jnp.zeros((SUBLANES - (CONV_WIDTH - 1), c), F32), prev, xbc_ref[...]], axis=0)
    act = _conv_taps(xp, cw_ref[...], cb_ref[...], dseq)
    pad = lambda v: jnp.concatenate([v, jnp.zeros((cl - dseq, v.shape[1]), F32)], axis=0)
    act = pad(act)
    st_ref[...] = st0_ref[...]
    y = _ssd_chunk(act[:, :d_ssm], act[:, d_ssm:d_ssm + gn], act[:, d_ssm + gn:], pad(dt_ref[...]),
                   pad(z_ref[...]), st_ref, dtb_ref[...], alog_ref[...], dskip_ref[...], normw_ref[...],
                   n_valid=dseq)
    y_ref[...] = y[:dseq]


def _ssd_decode(xbc, conv_prev, dt, z_src, z_col0, state0, conv_w, conv_b, dt_bias, a_log, d_skip,
                ssm_norm_w, *, dseq, d_ssm):
    m, c = xbc.shape
    n_b = m // dseq
    n_heads = d_ssm // SSM_HEAD_DIM
    st_rows = n_heads * SSM_HEAD_DIM
    cl = SSD_CHUNK
    tok = lambda b: (b, 0)
    fix = lambda b: (0, 0)
    st = lambda b: (b, 0, 0)
    assert z_col0 % d_ssm == 0
    kern = functools.partial(_ssd_decode_kernel, dseq=dseq, d_ssm=d_ssm, cl=cl)
    return pl.pallas_call(
        kern,
        out_shape=(jax.ShapeDtypeStruct((m, d_ssm), F32),
                   jax.ShapeDtypeStruct((n_b, st_rows, SSM_STATE), F32)),
        grid=(n_b,),
        in_specs=[pl.BlockSpec((dseq, c), tok),
                  pl.BlockSpec((None, CONV_WIDTH - 1, c), st),
                  pl.BlockSpec((dseq, LANES), tok),
                  pl.BlockSpec((dseq, d_ssm), lambda b: (b, z_col0 // d_ssm)),
                  pl.BlockSpec((None, st_rows, SSM_STATE), st),
                  pl.BlockSpec((CONV_WIDTH, c), fix), pl.BlockSpec((1, c), fix),
                  pl.BlockSpec((1, LANES), fix), pl.BlockSpec((1, LANES), fix),
                  pl.BlockSpec((1, LANES), fix), pl.BlockSpec((1, d_ssm), fix)],
        out_specs=(pl.BlockSpec((dseq, d_ssm), tok),
                   pl.BlockSpec((None, st_rows, SSM_STATE), st)),
        compiler_params=_cparams(("parallel",)),
        name="ssd_decode",
    )(xbc, conv_prev, dt, z_src, state0.reshape(n_b, st_rows, SSM_STATE), conv_w, conv_b.reshape(1, c),
      _pad_lanes(dt_bias), _pad_lanes(a_log), _pad_lanes(d_skip), ssm_norm_w.reshape(1, d_ssm))


def _layer(xp, xd, *, bp, seq, bd, dseq, pos_p, pos_d, wts, dec, lam_init):
    d_model = xp.shape[1]
    d_attn = d_model // 2
    d_ssm = d_model - d_attn
    n_heads = d_attn // ATTN_VHEAD
    n_kv = max(1, n_heads // KV_GROUP)
    d_q, d_k, d_v = n_heads * LANES, n_kv * LANES, n_kv * ATTN_VHEAD
    conv_dim = d_ssm + 2 * N_SSM_GROUPS * SSM_STATE
    c_z = d_q + d_k + d_v
    c_x = c_z + d_ssm
    c_dt = c_x + conv_dim
    n_ssm_heads = d_ssm // SSM_HEAD_DIM

    hp, hd = _rmsnorm(xp, wts["ln1_w"]), _rmsnorm(xd, wts["ln1_w"])
    proj_p, proj_d = _matmul_w32([hp], [hd], wts["w_in_t"], n_out=c_dt, w_is_transposed=True,
                                 name="in_proj")
    dt_p = _dt_proj(hp, wts["w_in_t"], c_dt, n_ssm_heads)
    dt_d = _dt_proj(hd, wts["w_in_t"], c_dt, n_ssm_heads)

    tab_p = _rope_tables(pos_p)
    tab_d = tuple(jnp.tile(t, (bd, 1)) for t in _rope_tables(pos_d))
    lam_rows = jnp.stack([wts["lambda_q1"], wts["lambda_k1"], wts["lambda_q2"], wts["lambda_k2"]])
    qk = dict(d_q=d_q, d_k=d_k, d_v=d_v)

    blk = _tile(seq, ATTN_BLOCK, LANES)
    qt, kp_f32, k0, k1, vp_f32, vt = _qkv_post(proj_p, tab_p, wts["q_norm_w"], wts["k_norm_w"],
                                               seq_blocks=seq // blk, **qk)
    o_p = _attn_prompt(qt, k0, k1, vt, lam_rows, wts["subln_w"], lam_init=lam_init)
    act = _conv_prompt(proj_p, c_x, wts["conv_w"], wts["conv_b"], batch=bp, seq=seq)
    y_p, ssm_p = _ssd_prompt(act, dt_p, proj_p, c_z, wts["dt_bias"], wts["a_log"], wts["d_skip"],
                             wts["ssm_norm_w"], batch=bp, seq=seq, d_ssm=d_ssm)
    conv_p = proj_p.reshape(bp, seq, c_dt)[:, seq - (CONV_WIDTH - 1):, c_x:]

    q_d, kd_f32, vd_f32 = _qkv_post(proj_d, tab_d, wts["q_norm_w"], wts["k_norm_w"], **qk)
    o_d = _attn_decode(q_d, kd_f32, vd_f32, dec["cache_k"], dec["cache_v"], dec["page_table"],
                       lam_rows, wts["subln_w"], page=dec["page"], dseq=dseq, lam_init=lam_init)
    xbc_d = proj_d[:, c_x:]
    y_d, ssm_d = _ssd_decode(xbc_d, dec["state_conv"], dt_d, proj_d[:, c_z:c_x], 0, dec["state_ssm"],
                             wts["conv_w"], wts["conv_b"], wts["dt_bias"], wts["a_log"], wts["d_skip"],
                             wts["ssm_norm_w"], dseq=dseq, d_ssm=d_ssm)
    conv_d = jnp.concatenate([dec["state_conv"], xbc_d.reshape(bd, dseq, conv_dim)],
                             axis=1)[:, -(CONV_WIDTH - 1):]

    x1_p, x1_d = _matmul_w32([o_p, y_p], [o_d, y_d], wts["w_out"], n_out=d_model, res=xp, resd=xd,
                             name="out_proj")
    h2_p, h2_d = _rmsnorm(x1_p, wts["ln2_w"]), _rmsnorm(x1_d, wts["ln2_w"])
    up_p, up_d = _matmul_w32([h2_p], [h2_d], wts["w_up"], n_out=wts["w_up"].shape[1], act="relu2",
                             out_dtype=BF16, tm=1024, name="ffn_up")
    d_ff = wts["w_down"].shape[0]
    kc = _tile(d_ff, FFN_DOWN_CHUNK, LANES)
    out_p, out_d = x1_p, x1_d
    for c in range(d_ff // kc):
        out_p, out_d = _matmul_w32([up_p], [up_d], wts["w_down"], n_out=d_model, res=out_p, resd=out_d,
                                   k_chunk=(c, kc), name="ffn_down")

    def heads(a, b, s, w):
        return a.reshape(b, s, n_kv, w)

    state = lambda a, b: a.reshape(b, n_ssm_heads, SSM_HEAD_DIM, SSM_STATE)
    return (out_p, out_d,
            (heads(kp_f32, bp, seq, LANES), heads(vp_f32, bp, seq, ATTN_VHEAD), state(ssm_p, bp), conv_p,
             heads(kd_f32, bd, dseq, LANES), heads(vd_f32, bd, dseq, ATTN_VHEAD), state(ssm_d, bd), conv_d))


def kernel(x_prompt, x_sample, cache_k, cache_v, state_ssm, state_conv, page_table, ln1_w, w_in, q_norm_w, k_norm_w, lambda_q1, lambda_k1, lambda_q2, lambda_k2, subln_w, conv_w, conv_b, dt_bias, a_log, d_skip, ssm_norm_w, w_out, ln2_w, w_up, w_down):
    depth = w_in.shape[0]
    bp, seq, d_model = x_prompt.shape
    bs, dseq, _ = x_sample.shape
    page = cache_k.shape[2]
    past_len = page_table.shape[1] * page
    pos_p = jnp.arange(seq, dtype=jnp.int32)
    pos_s = past_len + jnp.arange(dseq, dtype=jnp.int32)

    yp = x_prompt.reshape(bp * seq, d_model)
    ys = x_sample.reshape(bs * dseq, d_model)
    outs = [[] for _ in range(8)]
    for l in range(depth):
        lam_init = 0.8 - 0.6 * math.exp(-0.3 * l)
        w_in_t = w_in[l].T
        wts = dict(
            ln1_w=ln1_w[l], w_in_t=w_in_t,
            q_norm_w=q_norm_w[l], k_norm_w=k_norm_w[l],
            lambda_q1=lambda_q1[l], lambda_k1=lambda_k1[l], lambda_q2=lambda_q2[l], lambda_k2=lambda_k2[l],
            subln_w=subln_w[l], conv_w=conv_w[l], conv_b=conv_b[l], dt_bias=dt_bias[l], a_log=a_log[l],
            d_skip=d_skip[l], ssm_norm_w=ssm_norm_w[l], w_out=w_out[l],
            ln2_w=ln2_w[l], w_up=w_up[l], w_down=w_down[l])
        dec = dict(cache_k=cache_k[l].reshape(-1, cache_k.shape[-1]), cache_v=cache_v[l].reshape(-1, cache_v.shape[-1]),
                   page=page, page_table=page_table, state_conv=state_conv[l], state_ssm=state_ssm[l])
        yp, ys, layer_outs = _layer(yp, ys, bp=bp, seq=seq, bd=bs, dseq=dseq, pos_p=pos_p, pos_d=pos_s,
                                    wts=wts, dec=dec, lam_init=lam_init)
        for lst, val in zip(outs, layer_outs):
            lst.append(val)
    stacked = [jnp.stack(lst) for lst in outs]
    return (yp.reshape(bp, seq, d_model), ys.reshape(bs, dseq, d_model), *stacked)
```
